```python
import jax, jax.numpy as jnp
from jax import lax
import numpy as np

D_MODEL = 2048
BATCH = 8
SEQ = 2048
DEPTH = 1
DEC_BATCH = 32
DEC_SEQ = 1
PAST_LEN = 8192
PAGE_SIZE = 128

HEAD_DIM = 64
MIX_WIDTH = D_MODEL
N_ATTN_GROUPS = 3
ATTN_WINDOWS = (128, 512, 2048)
ATTN_DILATIONS = (1, 4, 16)
HPG = MIX_WIDTH // (4 * HEAD_DIM)
ATTN_HEADS = N_ATTN_GROUPS * HPG
ATTN_WIDTH = ATTN_HEADS * HEAD_DIM
SGU_GROUPS = (MIX_WIDTH - ATTN_WIDTH) // HEAD_DIM
SGU_WIDTH = SGU_GROUPS * HEAD_DIM
IN_WIDTH = 3 * ATTN_WIDTH + 2 * SGU_WIDTH
CHUNK = 128
ATTN_BLOCK = 128
ROT_DIM = HEAD_DIM // 4
ROPE_THETA = 500000.0
N_EXPERTS = 32
TOP_K = 4
D_FF = D_MODEL
SWIGLU_LIMIT = 7.0
SWIGLU_ALPHA = 1.702
MOE_BLOCK = 128
EPS = 1e-5

kernel_name = 'hymba_dilated_sgu_moe_step'


def rmsnorm(x, g):
    xf = x.astype(jnp.float32)
    y = xf * lax.rsqrt(jnp.mean(xf * xf, -1, keepdims=True) + EPS)
    return (y * g.astype(jnp.float32)).astype(x.dtype)


def layernorm(x, g, b):
    xf = x.astype(jnp.float32)
    mu = jnp.mean(xf, -1, keepdims=True)
    var = jnp.mean(jnp.square(xf - mu), -1, keepdims=True)
    y = (xf - mu) * lax.rsqrt(var + EPS)
    return (y * g.astype(jnp.float32) + b.astype(jnp.float32)).astype(x.dtype)


def rope(x, pos):
    half = ROT_DIM // 2
    inv = ROPE_THETA ** (-jnp.arange(0, ROT_DIM, 2, dtype=jnp.float32) / ROT_DIM)
    ang = pos.astype(jnp.float32)[:, None] * inv[None, :]
    cos = jnp.cos(ang)[None, :, None, :]
    sin = jnp.sin(ang)[None, :, None, :]
    xr = x[..., :ROT_DIM].astype(jnp.float32)
    x1, x2 = xr[..., :half], xr[..., half:]
    rot = jnp.concatenate([x1 * cos - x2 * sin, x2 * cos + x1 * sin], -1)
    return jnp.concatenate([rot.astype(x.dtype), x[..., ROT_DIM:]], -1)


def dilated_prompt(q, k, v, dilation, window):
    b, s, h, c = q.shape
    nk = window // dilation
    L = s // dilation
    nb = -(-L // ATTN_BLOCK)
    lp = nb * ATTN_BLOCK

    def split(t):
        t = t.reshape(b, L, dilation, h, c)
        t = jnp.pad(t, ((0, 0), (0, lp - L), (0, 0), (0, 0), (0, 0)))
        return t.reshape(b, nb, ATTN_BLOCK, dilation, h, c)

    def with_prev(t):
        prev = jnp.pad(t[:, :-1], ((0, 0), (1, 0), (0, 0), (0, 0), (0, 0), (0, 0)))
        return jnp.concatenate([prev, t], axis=2)

    qb = split(q)
    kw = with_prev(split(k))
    vw = with_prev(split(v)).astype(jnp.float32)
    sc = jnp.einsum('bnqrhc,bnkrhc->bnrhqk', qb, kw, preferred_element_type=jnp.float32) * (HEAD_DIM ** -0.5)
    qi = jnp.arange(ATTN_BLOCK)[:, None]
    kj = jnp.arange(2 * ATTN_BLOCK)[None, :]
    dist = qi + ATTN_BLOCK - kj
    blk = jnp.arange(nb)[:, None, None]
    mask = (dist >= 0) & (dist <= nk) & (blk * ATTN_BLOCK - ATTN_BLOCK + kj >= 0)
    sc = jnp.where(mask[None, :, None, None], sc, -jnp.inf)
    m = jnp.max(sc, -1, keepdims=True)
    p = jnp.exp(sc - m)
    den = jnp.sum(p, -1)
    lse = m[..., 0] + jnp.log(den)
    o = jnp.einsum('bnrhqk,bnkrhc->bnqrhc', p, vw) / den.transpose(0, 1, 4, 2, 3)[..., None]
    o = o.reshape(b, lp, dilation, h, c)[:, :L].reshape(b, s, h, c)
    lse = lse.transpose(0, 1, 4, 2, 3).reshape(b, lp, dilation, h)[:, :L].reshape(b, s, h)
    return o, lse


def dilated_sample(q, k_buf, v_buf, k_new, v_new, dilation, window):
    nk = window // dilation
    lb = k_buf.shape[1]
    t = q.shape[1]
    kc = jnp.concatenate([k_buf, k_new], 1)
    vc = jnp.concatenate([v_buf, v_new], 1)
    idx = lb + jnp.arange(t)[:, None] - dilation * jnp.arange(nk + 1)[None, :]
    valid = idx >= 0
    idx = jnp.maximum(idx, 0)
    kg = jnp.take(kc, idx, axis=1)
    vg = jnp.take(vc, idx, axis=1).astype(jnp.float32)
    sc = jnp.einsum('bthc,btjhc->bthj', q, kg, preferred_element_type=jnp.float32) * (HEAD_DIM ** -0.5)
    sc = jnp.where(valid[None, :, None, :], sc, -jnp.inf)
    m = jnp.max(sc, -1, keepdims=True)
    p = jnp.exp(sc - m)
    den = jnp.sum(p, -1)
    lse = m[..., 0] + jnp.log(den)
    o = jnp.einsum('bthj,btjhc->bthc', p, vg) / den[..., None]
    return o, lse


def combine_groups(outs, lses, dtype):
    alpha = jax.nn.softmax(jnp.stack(lses, 0), axis=0)
    return jnp.concatenate([o * alpha[g][..., None] for g, o in enumerate(outs)], axis=2).astype(dtype)


def chunk_spatial_gate(u, v, w_s, b_s):
    b, L, c = v.shape
    nc = -(-L // CHUNK)
    lp = nc * CHUNK
    vp = jnp.pad(v, ((0, 0), (0, lp - L), (0, 0))).reshape(b, nc, CHUNK, SGU_GROUPS, c // SGU_GROUPS)
    w = w_s * jnp.tril(jnp.ones((CHUNK, CHUNK), w_s.dtype))[None]
    f = jnp.einsum('gts,bnsgc->bntgc', w, vp) + b_s.T[:, :, None]
    f = f.reshape(b, lp, c)[:, :L]
    return u * f


def moe(x, w_router, b_router, w_up, b_up, w_down, b_down):
    n, d = x.shape
    logits = jnp.dot(x, w_router, preferred_element_type=jnp.float32) + b_router.astype(jnp.float32)
    top_v, top_i = lax.top_k(logits, TOP_K)
    gates = jax.nn.softmax(top_v, axis=-1)
    na = n * TOP_K
    flat_e = top_i.reshape(-1)
    flat_g = gates.reshape(-1)
    flat_t = jnp.repeat(jnp.arange(n, dtype=jnp.int32), TOP_K)
    order = jnp.argsort(flat_e, stable=True)
    se = flat_e[order]
    counts = jnp.bincount(flat_e, length=N_EXPERTS)
    starts = jnp.cumsum(counts) - counts
    padded = (counts + MOE_BLOCK - 1) // MOE_BLOCK * MOE_BLOCK
    pends = jnp.cumsum(padded)
    pstarts = pends - padded
    dest = pstarts[se] + jnp.arange(na) - starts[se]
    nb = -(-na // MOE_BLOCK) + N_EXPERTS
    rows = nb * MOE_BLOCK
    row_tok = jnp.zeros((rows,), jnp.int32).at[dest].set(flat_t[order])
    row_gate = jnp.zeros((rows,), jnp.float32).at[dest].set(flat_g[order])
    block_e = jnp.minimum(jnp.searchsorted(pends, jnp.arange(nb) * MOE_BLOCK, side='right'), N_EXPERTS - 1)
    xb = x[row_tok].reshape(nb, MOE_BLOCK, d)

    def expert_block(args):
        xr, e = args
        hh = jnp.dot(xr, w_up[e], preferred_element_type=jnp.float32) + b_up[e].astype(jnp.float32)
        glu = jnp.minimum(hh[:, :D_FF], SWIGLU_LIMIT)
        lin = jnp.clip(hh[:, D_FF:], -SWIGLU_LIMIT, SWIGLU_LIMIT)
        a = glu * jax.nn.sigmoid(SWIGLU_ALPHA * glu) * (lin + 1.0)
        return jnp.dot(a.astype(xr.dtype), w_down[e], preferred_element_type=jnp.float32) + b_down[e].astype(jnp.float32)

    yb = lax.map(expert_block, (xb, block_e)).reshape(rows, d)
    y = jax.ops.segment_sum(yb * row_gate[:, None], row_tok, num_segments=n)
    return y.astype(x.dtype)


def decoder_layer(x, bp, sp, bs, ss, states, norm_mix_g, w_in, ln_va_g, ln_va_b, w_spatial, b_spatial,
                  norm_out_b_g, norm_out_a_g, w_out, norm_ffn_g, w_router, b_router, w_up, b_up, w_down, b_down):
    npr = bp * sp
    h = rmsnorm(x, norm_mix_g)
    z = h @ w_in
    a = ATTN_WIDTH
    q, k, v = z[:, :a], z[:, a:2 * a], z[:, 2 * a:3 * a]
    u = jax.nn.gelu(z[:, 3 * a:3 * a + SGU_WIDTH], approximate=False)
    va = layernorm(jax.nn.gelu(z[:, 3 * a + SGU_WIDTH:], approximate=False), ln_va_g, ln_va_b)
    pos_p = jnp.arange(sp)
    pos_s = PAST_LEN + jnp.arange(ss)

    def heads(t, b_, s_):
        return t.reshape(b_, s_, ATTN_HEADS, HEAD_DIM)

    qp, kp, vp = rope(heads(q[:npr], bp, sp), pos_p), rope(heads(k[:npr], bp, sp), pos_p), heads(v[:npr], bp, sp)
    qs, ks, vs = rope(heads(q[npr:], bs, ss), pos_s), rope(heads(k[npr:], bs, ss), pos_s), heads(v[npr:], bs, ss)
    op_l, lp_l, os_l, ls_l, kv_p, kv_s = [], [], [], [], [], []
    for g in range(N_ATTN_GROUPS):
        sl = slice(g * HPG, (g + 1) * HPG)
        win, dil = ATTN_WINDOWS[g], ATTN_DILATIONS[g]
        o_, l_ = dilated_prompt(qp[:, :, sl], kp[:, :, sl], vp[:, :, sl], dil, win)
        op_l.append(o_)
        lp_l.append(l_)
        lw = min(win, sp)
        kv_p.append(jnp.stack([kp[:, sp - lw:, sl], vp[:, sp - lw:, sl]], axis=2))
        st = states[g]
        o_, l_ = dilated_sample(qs[:, :, sl], st[:, :, 0], st[:, :, 1], ks[:, :, sl], vs[:, :, sl], dil, win)
        os_l.append(o_)
        ls_l.append(l_)
        kv_s.append(jnp.stack([ks[:, :, sl], vs[:, :, sl]], axis=2))
    attn_p = combine_groups(op_l, lp_l, x.dtype).reshape(npr, a)
    attn_s = combine_groups(os_l, ls_l, x.dtype).reshape(bs * ss, a)
    attn = jnp.concatenate([attn_p, attn_s], 0)
    sgu_p = chunk_spatial_gate(u[:npr].reshape(bp, sp, SGU_WIDTH), va[:npr].reshape(bp, sp, SGU_WIDTH), w_spatial, b_spatial)
    sgu_s = chunk_spatial_gate(u[npr:].reshape(bs, ss, SGU_WIDTH), va[npr:].reshape(bs, ss, SGU_WIDTH), w_spatial, b_spatial)
    sgu = jnp.concatenate([sgu_p.reshape(npr, SGU_WIDTH), sgu_s.reshape(bs * ss, SGU_WIDTH)], 0)
    mixed = jnp.concatenate([rmsnorm(attn, norm_out_b_g), rmsnorm(sgu, norm_out_a_g)], -1)
    x = x + mixed @ w_out
    x = x + moe(rmsnorm(x, norm_ffn_g), w_router, b_router, w_up, b_up, w_down, b_down)
    return x, kv_p, kv_s, va[npr:].reshape(bs, ss, SGU_WIDTH)


def setup_inputs(seed: int = 0) -> dict:
    key = jax.random.key(seed)
    ks = jax.random.split(key, 24)

    def nrm(k, shape, scale):
        return scale * jax.random.normal(k, shape, jnp.float32)

    lbs = [min(w, PAST_LEN) for w in ATTN_WINDOWS]
    return {
        'x_prompt': nrm(ks[0], (BATCH, SEQ, D_MODEL), 1.0),
        'x_sample': nrm(ks[1], (DEC_BATCH, DEC_SEQ, D_MODEL), 1.0),
        'state_b0_kv': nrm(ks[2], (DEPTH, DEC_BATCH, lbs[0], 2, HPG, HEAD_DIM), 1.0),
        'state_b1_kv': nrm(ks[3], (DEPTH, DEC_BATCH, lbs[1], 2, HPG, HEAD_DIM), 1.0),
        'state_b2_kv': nrm(ks[4], (DEPTH, DEC_BATCH, lbs[2], 2, HPG, HEAD_DIM), 1.0),
        'norm_mix_g': 1.0 + nrm(ks[5], (DEPTH, D_MODEL), 0.02),
        'w_in': nrm(ks[6], (DEPTH, D_MODEL, IN_WIDTH), D_MODEL ** -0.5),
        'ln_va_g': 1.0 + nrm(ks[7], (DEPTH, SGU_WIDTH), 0.02),
        'ln_va_b': nrm(ks[8], (DEPTH, SGU_WIDTH), 0.02),
        'w_spatial': nrm(ks[9], (DEPTH, SGU_GROUPS, CHUNK, CHUNK), CHUNK ** -0.5),
        'b_spatial': 1.0 + nrm(ks[10], (DEPTH, SGU_GROUPS, CHUNK), 0.02),
        'norm_out_b_g': 1.0 + nrm(ks[11], (DEPTH, ATTN_WIDTH), 0.02),
        'norm_out_a_g': 1.0 + nrm(ks[12], (DEPTH, SGU_WIDTH), 0.02),
        'w_out': nrm(ks[13], (DEPTH, MIX_WIDTH, D_MODEL), MIX_WIDTH ** -0.5),
        'norm_ffn_g': 1.0 + nrm(ks[14], (DEPTH, D_MODEL), 0.02),
        'w_router': nrm(ks[15], (DEPTH, D_MODEL, N_EXPERTS), D_MODEL ** -0.5),
        'b_router': nrm(ks[16], (DEPTH, N_EXPERTS), 0.01),
        'w_up': nrm(ks[17], (DEPTH, N_EXPERTS, D_MODEL, 2 * D_FF), D_MODEL ** -0.5),
        'b_up': nrm(ks[18], (DEPTH, N_EXPERTS, 2 * D_FF), 0.02),
        'w_down': nrm(ks[19], (DEPTH, N_EXPERTS, D_FF, D_MODEL), D_FF ** -0.5),
        'b_down': nrm(ks[20], (DEPTH, N_EXPERTS, D_MODEL), 0.02),
        'norm_final_g': 1.0 + nrm(ks[21], (D_MODEL,), 0.02),
    }


def reference(x_prompt, x_sample, state_b0_kv, state_b1_kv, state_b2_kv, norm_mix_g, w_in, ln_va_g, ln_va_b,
              w_spatial, b_spatial, norm_out_b_g, norm_out_a_g, w_out, norm_ffn_g, w_router, b_router,
              w_up, b_up, w_down, b_down, norm_final_g):
    bp, sp, d = x_prompt.shape
    bs, ss, _ = x_sample.shape
    x = jnp.concatenate([x_prompt.reshape(bp * sp, d), x_sample.reshape(bs * ss, d)], 0)
    kv_prompt = [[] for _ in range(N_ATTN_GROUPS)]
    kv_sample = [[] for _ in range(N_ATTN_GROUPS)]
    v_sgu = []
    for l in range(DEPTH):
        x, kv_p, kv_s, va_s = decoder_layer(
            x, bp, sp, bs, ss, (state_b0_kv[l], state_b1_kv[l], state_b2_kv[l]),
            norm_mix_g[l], w_in[l], ln_va_g[l], ln_va_b[l], w_spatial[l], b_spatial[l],
            norm_out_b_g[l], norm_out_a_g[l], w_out[l], norm_ffn_g[l], w_router[l], b_router[l],
            w_up[l], b_up[l], w_down[l], b_down[l])
        for g in range(N_ATTN_GROUPS):
            kv_prompt[g].append(kv_p[g])
            kv_sample[g].append(kv_s[g])
        v_sgu.append(va_s)
    y = rmsnorm(x, norm_final_g)
    y_prompt = y[:bp * sp].reshape(bp, sp, d)
    y_sample = y[bp * sp:].reshape(bs, ss, d)
    kv_prompt_b0, kv_prompt_b1, kv_prompt_b2 = [jnp.stack(t, 0) for t in kv_prompt]
    kv_sample_b0, kv_sample_b1, kv_sample_b2 = [jnp.stack(t, 0) for t in kv_sample]
    v_sample_sgu = jnp.stack(v_sgu, 0)
    return (y_prompt, y_sample, kv_prompt_b0, kv_prompt_b1, kv_prompt_b2, kv_sample_b0, kv_sample_b1, kv_sample_b2, v_sample_sgu)
```

```python
import functools

import jax
import jax.numpy as jnp
import numpy as np
from jax import lax
from jax.experimental import pallas as pl
from jax.experimental.pallas import tpu as pltpu

F32 = jnp.float32
BF16 = jnp.bfloat16
U32 = jnp.uint32

HEAD_DIM = 64
ROT_DIM = HEAD_DIM // 4
ROPE_THETA = 500000.0
ATTN_WINDOWS = (128, 512, 2048)
ATTN_DILATIONS = (1, 4, 16)
N_GROUPS = 3
CHUNK = 128
ATTN_BLOCK = 128
TOP_K = 4
SWIGLU_LIMIT = 7.0
SWIGLU_ALPHA = 1.702
EPS = 1e-5

LANES = 128
VMEM_LIMIT = 56 * 1024 * 1024

MOE_TM = 1280
MOE_SUB = 256
MOE_TF = 512


def _cparams(sem):
    return pltpu.CompilerParams(dimension_semantics=sem, vmem_limit_bytes=VMEM_LIMIT)


def _gelu(x):
    return 0.5 * x * (1.0 + lax.erf(x * np.float32(np.sqrt(0.5))))


def _split3(a):
    a1 = a.astype(BF16)
    r1 = a - a1.astype(F32)
    a2 = r1.astype(BF16)
    a3 = (r1 - a2.astype(F32)).astype(BF16)
    return a1, a2, a3


def _dot3(a, b_bf16):
    a1, a2, a3 = _split3(a)
    d = functools.partial(jnp.dot, preferred_element_type=F32)
    return d(a1, b_bf16) + d(a2, b_bf16) + d(a3, b_bf16)


def _in_proj_kernel(x_ref, g_ref, w_ref, cos_ref, sa_ref, sb_ref, lng_ref, lnb_ref,
                    qkv_ref, kvf_ref, u_ref, va_ref, xn_sc, *, nq):
    j = pl.program_id(1)

    @pl.when(j == 0)
    def _():
        x = x_ref[...]
        ms = jnp.mean(x * x, axis=-1, keepdims=True)
        xn_sc[...] = (x * lax.rsqrt(ms + EPS) * g_ref[...]).astype(BF16)

    acc = jnp.dot(xn_sc[...], w_ref[...], preferred_element_type=F32)

    @pl.when(j < 2 * nq)
    def _():
        cos, sa, sb = cos_ref[...], sa_ref[...], sb_ref[...]
        parts = []
        for s in range(acc.shape[1] // LANES):
            z = acc[:, s * LANES:(s + 1) * LANES]
            parts.append(z * cos + pltpu.roll(z, ROT_DIM // 2, 1) * sa
                         + pltpu.roll(z, LANES - ROT_DIM // 2, 1) * sb)
        r = jnp.concatenate(parts, axis=1)
        qkv_ref[...] = r.astype(BF16)

        @pl.when(j >= nq)
        def _():
            kvf_ref[...] = r

    @pl.when((j >= 2 * nq) & (j < 3 * nq))
    def _():
        qkv_ref[...] = acc.astype(BF16)
        kvf_ref[...] = acc

    @pl.when(j == 3 * nq)
    def _():
        u_ref[...] = _gelu(acc).astype(BF16)

    @pl.when(j == 3 * nq + 1)
    def _():
        t = _gelu(acc)
        mu = jnp.mean(t, axis=-1, keepdims=True)
        var = jnp.mean(jnp.square(t - mu), axis=-1, keepdims=True)
        va_ref[...] = (t - mu) * lax.rsqrt(var + EPS) * lng_ref[...] + lnb_ref[...]


def _in_proj(x, g, w_bf16, cos_t, sa_t, sb_t, ln_g, ln_b, *, tm, attn_w, sgu_w):
    n, d = x.shape
    bw = sgu_w
    nq = attn_w // bw
    nj = 3 * nq + 2
    assert w_bf16.shape[1] == nj * bw
    row = lambda i, j: (i, 0)
    return pl.pallas_call(
        functools.partial(_in_proj_kernel, nq=nq),
        grid=(pl.cdiv(n, tm), nj),
        in_specs=[
            pl.BlockSpec((tm, d), row),
            pl.BlockSpec((1, d), lambda i, j: (0, 0)),
            pl.BlockSpec((d, bw), lambda i, j: (0, j)),
            pl.BlockSpec((tm, LANES), row),
            pl.BlockSpec((tm, LANES), row),
            pl.BlockSpec((tm, LANES), row),
            pl.BlockSpec((1, bw), lambda i, j: (0, 0)),
            pl.BlockSpec((1, bw), lambda i, j: (0, 0)),
        ],
        out_specs=[
            pl.BlockSpec((tm, bw), lambda i, j: (i, jnp.minimum(j, 3 * nq - 1))),
            pl.BlockSpec((tm, bw), lambda i, j: (i, jnp.clip(j - nq, 0, 2 * nq - 1))),
            pl.BlockSpec((tm, bw), row),
            pl.BlockSpec((tm, bw), row),
        ],
        out_shape=[
            jax.ShapeDtypeStruct((n, 3 * attn_w), BF16),
            jax.ShapeDtypeStruct((n, 2 * attn_w), F32),
            jax.ShapeDtypeStruct((n, sgu_w), BF16),
            jax.ShapeDtypeStruct((n, sgu_w), F32),
        ],
        scratch_shapes=[pltpu.VMEM((tm, d), BF16)],
        compiler_params=_cparams(("arbitrary", "arbitrary")),
        name="in_proj",
    )(x, g, w_bf16, cos_t, sa_t, sb_t, ln_g, ln_b)


def _rope_tables(pos):
    half = ROT_DIM // 2
    inv = ROPE_THETA ** (-jnp.arange(0, ROT_DIM, 2, dtype=F32) / ROT_DIM)
    ang = pos.astype(F32)[:, None] * inv[None, :]
    cos, sin = jnp.cos(ang), jnp.sin(ang)
    rows = pos.shape[0]
    one = jnp.ones((rows, HEAD_DIM - ROT_DIM), F32)
    zero = jnp.zeros((rows, HEAD_DIM - ROT_DIM), F32)
    zh = jnp.zeros((rows, half), F32)
    cos_h = jnp.concatenate([cos, cos, one], axis=1)
    sa_h = jnp.concatenate([zh, sin, zero], axis=1)
    sb_h = jnp.concatenate([-sin, zh, zero], axis=1)
    rep = LANES // HEAD_DIM
    return tuple(jnp.tile(t, (1, rep)) for t in (cos_h, sa_h, sb_h))


def _attn_prompt_kernel(q_ref, kp_ref, kc_ref, vp_ref, vc_ref, o_ref, lse_ref, *, nk, hpg):
    n = pl.program_id(2)
    q = q_ref[...]
    k = jnp.concatenate([kp_ref[...], kc_ref[...]], axis=0)
    v = jnp.concatenate([vp_ref[...], vc_ref[...]], axis=0)
    tq, tk = q.shape[0], k.shape[0]
    qi = lax.broadcasted_iota(jnp.int32, (tq, tk), 0)
    kj = lax.broadcasted_iota(jnp.int32, (tq, tk), 1)
    dist = qi + tq - kj
    valid = (dist >= 0) & (dist <= nk) & ((n - 1) * tq + kj >= 0)
    outs, lses = [], []
    for h in range(hpg):
        sl = slice(h * HEAD_DIM, (h + 1) * HEAD_DIM)
        s = lax.dot_general(q[:, sl], k[:, sl], (((1,), (1,)), ((), ())),
                            preferred_element_type=F32) * (HEAD_DIM ** -0.5)
        s = jnp.where(valid, s, -jnp.inf)
        m = jnp.max(s, axis=-1, keepdims=True)
        p = jnp.exp(s - m)
        den = jnp.sum(p, axis=-1, keepdims=True)
        o = jnp.dot(p.astype(BF16), v[:, sl], preferred_element_type=F32) / den
        outs.append(o)
        lses.append(jnp.broadcast_to(m + jnp.log(den), (tq, HEAD_DIM)))
    o_ref[...] = jnp.concatenate(outs, axis=1).astype(BF16)
    lse_ref[...] = jnp.concatenate(lses, axis=1)


def _attn_prompt(qkv, g, *, bp, sp, attn_w, gw):
    dil, win = ATTN_DILATIONS[g], ATTN_WINDOWS[g]
    nk = win // dil
    assert nk <= ATTN_BLOCK and sp % (dil * ATTN_BLOCK) == 0
    L = sp // dil
    nb = L // ATTN_BLOCK
    ncol = 3 * attn_w // gw
    nq = attn_w // gw
    view = qkv.reshape(bp, L, dil * 3 * attn_w)
    blk = (None, ATTN_BLOCK, gw)

    def spec(col, prev):
        if prev:
            return pl.BlockSpec(blk, lambda b, r, n: (b, jnp.maximum(n - 1, 0), r * ncol + col))
        return pl.BlockSpec(blk, lambda b, r, n: (b, n, r * ncol + col))

    out_spec = pl.BlockSpec(blk, lambda b, r, n: (b, n, r))
    o, lse = pl.pallas_call(
        functools.partial(_attn_prompt_kernel, nk=nk, hpg=gw // HEAD_DIM),
        grid=(bp, dil, nb),
        in_specs=[spec(g, False), spec(nq + g, True), spec(nq + g, False),
                  spec(2 * nq + g, True), spec(2 * nq + g, False)],
        out_specs=[out_spec, out_spec],
        out_shape=[jax.ShapeDtypeStruct((bp, L, dil * gw), BF16),
                   jax.ShapeDtypeStruct((bp, L, dil * gw), F32)],
        compiler_params=_cparams(("arbitrary", "arbitrary", "arbitrary")),
        name=f"attn_prompt_g{g}",
    )(view, view, view, view, view)
    return o.reshape(bp * sp, gw), lse.reshape(bp * sp, gw)


def _attn_sample_kernel(q_ref, kvn_ref, s0_ref, s1_ref, s2_ref, o_ref, lse_ref, *, attn_w, gw):
    hpg = gw // HEAD_DIM
    li = lax.broadcasted_iota(jnp.int32, (gw, LANES), 0) // HEAD_DIM
    hi = lax.broadcasted_iota(jnp.int32, (gw, LANES), 1)
    e_mat = jnp.where(li == hi, 1.0, 0.0).astype(BF16)
    hj = lax.broadcasted_iota(jnp.int32, (LANES, gw), 0)
    lj = lax.broadcasted_iota(jnp.int32, (LANES, gw), 1) // HEAD_DIM
    et_mat = jnp.where(hj == lj, 1.0, 0.0).astype(BF16)
    scale = HEAD_DIM ** -0.5
    q_all = q_ref[...].astype(F32)
    kvn = kvn_ref[...]
    o_parts, lse_parts = [], []
    for g, s_ref in enumerate((s0_ref, s1_ref, s2_ref)):
        q = q_all[:, g * gw:(g + 1) * gw]
        k_new = kvn[:, g * gw:(g + 1) * gw]
        v_new = kvn[:, attn_w + g * gw:attn_w + (g + 1) * gw]
        ks = s_ref[:, :gw]
        vs = s_ref[:, gw:]
        s_buf = _dot3(ks * q, e_mat) * scale
        s_new = _dot3(jnp.broadcast_to(k_new * q, (8, gw)), e_mat)[0:1] * scale
        m = jnp.maximum(jnp.max(s_buf, axis=0, keepdims=True), s_new)
        p_buf = jnp.exp(s_buf - m)
        p_new = jnp.exp(s_new - m)
        den = jnp.sum(p_buf, axis=0, keepdims=True) + p_new
        lse = m + jnp.log(den)
        small = jnp.concatenate([den, lse, p_new, jnp.zeros((5, LANES), F32)], axis=0)
        small_e = _dot3(small, et_mat)
        pe = _dot3(p_buf, et_mat)
        num = jnp.sum(pe * vs, axis=0, keepdims=True) + small_e[2:3] * v_new
        o_parts.append(num / small_e[0:1])
        lse_parts.append(small_e[1:2])
    del hpg
    o_ref[...] = jnp.concatenate(o_parts, axis=1)
    lse_ref[...] = jnp.concatenate(lse_parts, axis=1)


def _attn_sample(qkv_s, kvf_s, states, *, attn_w, gw):
    bs = qkv_s.shape[0]
    views = []
    for g, st in enumerate(states):
        dil, win = ATTN_DILATIONS[g], ATTN_WINDOWS[g]
        nk = win // dil
        lb = st.shape[1]
        assert lb == win and nk == ATTN_BLOCK and st.shape[0] == bs
        views.append(st.reshape(bs, nk, dil * 2 * gw))
    st_spec = pl.BlockSpec((None, ATTN_BLOCK, 2 * gw), lambda b: (b, 0, 0))
    row3 = lambda w: pl.BlockSpec((None, 1, w), lambda b: (b, 0, 0))
    o, lse = pl.pallas_call(
        functools.partial(_attn_sample_kernel, attn_w=attn_w, gw=gw),
        grid=(bs,),
        in_specs=[row3(3 * attn_w), row3(2 * attn_w), st_spec, st_spec, st_spec],
        out_specs=[row3(attn_w), row3(attn_w)],
        out_shape=[jax.ShapeDtypeStruct((bs, 1, attn_w), F32),
                   jax.ShapeDtypeStruct((bs, 1, attn_w), F32)],
        compiler_params=_cparams(("arbitrary",)),
        name="attn_sample",
    )(qkv_s.reshape(bs, 1, 3 * attn_w), kvf_s.reshape(bs, 1, 2 * attn_w), *views)
    return o.reshape(bs, attn_w), lse.reshape(bs, attn_w)


def _sgu_kernel(u_ref, va_ref, w_ref, bias_ref, g_ref, o_ref, *, cps):
    ng = w_ref.shape[0]
    ti = lax.broadcasted_iota(jnp.int32, (CHUNK, CHUNK), 0)
    si = lax.broadcasted_iota(jnp.int32, (CHUNK, CHUNK), 1)
    wt = [jnp.where(ti >= si, w_ref[g], 0.0).astype(BF16) for g in range(ng)]
    cw = va_ref.shape[1] // ng
    for c in range(cps):
        rows = slice(c * CHUNK, (c + 1) * CHUNK)
        va = va_ref[rows, :].astype(BF16)
        f = jnp.concatenate(
            [jnp.dot(wt[g], va[:, g * cw:(g + 1) * cw], preferred_element_type=F32) for g in range(ng)],
            axis=1) + bias_ref[...]
        s = u_ref[rows, :].astype(F32) * f
        ms = jnp.mean(s * s, axis=-1, keepdims=True)
        o_ref[rows, :] = (s * lax.rsqrt(ms + EPS) * g_ref[...]).astype(BF16)


def _sgu_prompt(u, va, w_sp, bias_tc, g_a, *, cps):
    n, w = u.shape
    tm = cps * CHUNK
    assert n % tm == 0
    row = lambda i: (i, 0)
    return pl.pallas_call(
        functools.partial(_sgu_kernel, cps=cps),
        grid=(n // tm,),
        in_specs=[pl.BlockSpec((tm, w), row), pl.BlockSpec((tm, w), row),
                  pl.BlockSpec(w_sp.shape, lambda i: (0, 0, 0)),
                  pl.BlockSpec((CHUNK, w), lambda i: (0, 0)),
                  pl.BlockSpec((1, w), lambda i: (0, 0))],
        out_specs=pl.BlockSpec((tm, w), row),
        out_shape=jax.ShapeDtypeStruct((n, w), BF16),
        compiler_params=_cparams(("arbitrary",)),
        name="sgu_prompt",
    )(u, va, w_sp, bias_tc, g_a)


def _sgu_sample_kernel(u_ref, va_ref, w0_ref, b0_ref, g_ref, o_ref):
    s = u_ref[...].astype(F32) * (va_ref[...] * w0_ref[...] + b0_ref[...])
    ms = jnp.mean(s * s, axis=-1, keepdims=True)
    o_ref[...] = (s * lax.rsqrt(ms + EPS) * g_ref[...]).astype(BF16)


def _sgu_sample(u, va, w0, b0, g_a):
    n, w = u.shape
    full = pl.BlockSpec((n, w), lambda i: (0, 0))
    vec = pl.BlockSpec((1, w), lambda i: (0, 0))
    return pl.pallas_call(
        _sgu_sample_kernel, grid=(1,),
        in_specs=[full, full, vec, vec, vec], out_specs=full,
        out_shape=jax.ShapeDtypeStruct((n, w), BF16),
        name="sgu_sample",
    )(u, va, w0, b0, g_a)


def _out_proj_kernel(*refs, gw, aliased):
    if aliased:
        refs = refs[:9] + refs[12:]
    (o_ref, lse_ref, sgu_ref, x_ref, gb_ref, wout_ref, gffn_ref, wr_ref, br_ref,
     x2_ref, xnp_ref, lg_ref) = refs
    lse = lse_ref[...]
    ls = [lse[:, g * gw:(g + 1) * gw] for g in range(N_GROUPS)]
    m = jnp.maximum(jnp.maximum(ls[0], ls[1]), ls[2])
    es = [jnp.exp(l - m) for l in ls]
    tot = es[0] + es[1] + es[2]
    o = o_ref[...].astype(F32)
    attn = jnp.concatenate([o[:, g * gw:(g + 1) * gw] * (es[g] / tot) for g in range(N_GROUPS)], axis=1)
    ms = jnp.mean(attn * attn, axis=-1, keepdims=True)
    attn_n = (attn * lax.rsqrt(ms + EPS) * gb_ref[...]).astype(BF16)
    aw = attn_n.shape[1]
    y = jnp.dot(attn_n, wout_ref[:aw, :], preferred_element_type=F32)
    y = y + jnp.dot(sgu_ref[...], wout_ref[aw:, :], preferred_element_type=F32)
    x2 = x_ref[...] + y
    x2_ref[...] = x2
    ms2 = jnp.mean(x2 * x2, axis=-1, keepdims=True)
    xn = x2 * lax.rsqrt(ms2 + EPS) * gffn_ref[...]
    x1 = xn.astype(BF16)
    xr = (xn - x1.astype(F32)).astype(BF16)
    d = functools.partial(jnp.dot, preferred_element_type=F32)
    lg_ref[...] = d(x1, wr_ref[0]) + d(x1, wr_ref[1]) + d(xr, wr_ref[0]) + br_ref[...]
    bits = lax.bitcast_convert_type(x1.astype(F32), U32)
    half = bits.shape[1] // 2
    xnp_ref[...] = bits[:, :half] | (bits[:, half:] >> 16)


def _out_proj(o, lse, sgu_n, x, g_b, wout_bf16, g_ffn, wr_hl, b_r, *, tm, n_total, row0, prev=None, gw):
    n, d = x.shape
    aw = o.shape[1]
    assert n % tm == 0 and row0 % tm == 0
    off = row0 // tm
    row = lambda i: (i, 0)
    orow = lambda i: (i + off, 0)
    const2 = lambda i: (0, 0)
    in_specs = [
        pl.BlockSpec((tm, aw), row), pl.BlockSpec((tm, aw), row), pl.BlockSpec((tm, d - aw), row),
        pl.BlockSpec((tm, d), row), pl.BlockSpec((1, aw), const2),
        pl.BlockSpec((d, d), const2), pl.BlockSpec((1, d), const2),
        pl.BlockSpec((2, d, LANES), lambda i: (0, 0, 0)), pl.BlockSpec((1, LANES), const2),
    ]
    args = [o, lse, sgu_n, x, g_b, wout_bf16, g_ffn, wr_hl, b_r]
    aliases = {}
    if prev is not None:
        in_specs += [pl.BlockSpec(memory_space=pl.ANY)] * 3
        args += list(prev)
        aliases = {9: 0, 10: 1, 11: 2}
    return pl.pallas_call(
        functools.partial(_out_proj_kernel, gw=gw, aliased=prev is not None),
        grid=(n // tm,),
        in_specs=in_specs,
        out_specs=[pl.BlockSpec((tm, d), orow), pl.BlockSpec((tm, d // 2), orow),
                   pl.BlockSpec((tm, LANES), orow)],
        out_shape=[jax.ShapeDtypeStruct((n_total, d), F32),
                   jax.ShapeDtypeStruct((n_total, d // 2), U32),
                   jax.ShapeDtypeStruct((n_total, LANES), F32)],
        input_output_aliases=aliases,
        compiler_params=_cparams(("arbitrary",)),
        name="out_proj",
    )(*args)


def _moe_kernel(be_ref, nr_ref, tok_ref, dst_ref, gate_ref, xn_hbm,
                wg_ref, wl_ref, bg_ref, bl_ref, wd_ref, bd_ref, out_hbm,
                xb_sc, acc_sc, wg_sc, wl_sc, wd_sc, sem_in, sem_out, *, nc):
    b = pl.program_id(0)
    c = pl.program_id(1)
    nr = nr_ref[b]
    del be_ref

    def row_in(i):
        return pltpu.make_async_copy(xn_hbm.at[pl.ds(tok_ref[0, i], 1), :], xb_sc.at[pl.ds(i, 1), :], sem_in)

    def row_out(i):
        return pltpu.make_async_copy(acc_sc.at[pl.ds(i, 1), :], out_hbm.at[pl.ds(dst_ref[0, i], 1), :], sem_out)

    @pl.when(nr > 0)
    def _():
        @pl.when(c == 0)
        def _():
            def start(i, carry):
                row_in(i).start()
                return carry
            lax.fori_loop(0, nr, start, 0)

            def wait(i, carry):
                row_in(i).wait()
                return carry
            lax.fori_loop(0, nr, wait, 0)

        wg_sc[...] = wg_ref[...].astype(BF16)
        wl_sc[...] = wl_ref[...].astype(BF16)
        wd_sc[...] = wd_ref[...].astype(BF16)
        half = xb_sc.shape[1]

        def sub(s, carry):
            r0 = pl.multiple_of(s * MOE_SUB, MOE_SUB)
            rows = pl.ds(r0, MOE_SUB)
            w = xb_sc[rows, :]
            x_hi = lax.bitcast_convert_type(w & jnp.uint32(0xFFFF0000), F32).astype(BF16)
            x_lo = lax.bitcast_convert_type(w << 16, F32).astype(BF16)
            d = functools.partial(jnp.dot, preferred_element_type=F32)
            hg = d(x_hi, wg_sc[:half, :]) + d(x_lo, wg_sc[half:, :]) + bg_ref[...]
            hl = d(x_hi, wl_sc[:half, :]) + d(x_lo, wl_sc[half:, :]) + bl_ref[...]
            glu = jnp.minimum(hg, SWIGLU_LIMIT)
            lin = jnp.clip(hl, -SWIGLU_LIMIT, SWIGLU_LIMIT)
            a = glu * jax.nn.sigmoid(SWIGLU_ALPHA * glu) * (lin + 1.0)
            y = d(a.astype(BF16), wd_sc[...])

            @pl.when(c == 0)
            def _():
                acc_sc[rows, :] = y + bd_ref[...]

            @pl.when((c > 0) & (c < nc - 1))
            def _():
                acc_sc[rows, :] += y

            @pl.when(c == nc - 1)
            def _():
                acc_sc[rows, :] = (acc_sc[rows, :] + y) * gate_ref[rows, :]
            return carry

        lax.fori_loop(0, (nr + MOE_SUB - 1) // MOE_SUB, sub, 0)

        @pl.when(c == nc - 1)
        def _():
            def start(i, carry):
                row_out(i).start()
                return carry
            lax.fori_loop(0, nr, start, 0)

            def wait(i, carry):
                row_out(i).wait()
                return carry
            lax.fori_loop(0, nr, wait, 0)


def _moe(xnp, sched, w_up, b_up, w_down, b_down):
    be, nrows, tok_tbl, dst_tbl, gate_tbl = sched
    n_tok, half = xnp.shape
    d = 2 * half
    n_exp, _, ff2 = w_up.shape
    ff = ff2 // 2
    assert ff % MOE_TF == 0 and MOE_TM % MOE_SUB == 0 and MOE_TF % LANES == 0
    nc = ff // MOE_TF
    nbmax = be.shape[0]

    def cidx(b, c, nr):
        return jnp.where(nr[b] > 0, c, nc - 1)

    grid_spec = pltpu.PrefetchScalarGridSpec(
        num_scalar_prefetch=2,
        grid=(nbmax, nc),
        in_specs=[
            pl.BlockSpec((None, 1, MOE_TM), lambda b, c, be, nr: (b, 0, 0), memory_space=pltpu.SMEM),
            pl.BlockSpec((None, 1, MOE_TM), lambda b, c, be, nr: (b, 0, 0), memory_space=pltpu.SMEM),
            pl.BlockSpec((None, MOE_TM, 1), lambda b, c, be, nr: (b, 0, 0)),
            pl.BlockSpec(memory_space=pl.ANY),
            pl.BlockSpec((None, d, MOE_TF), lambda b, c, be, nr: (be[b], 0, cidx(b, c, nr))),
            pl.BlockSpec((None, d, MOE_TF), lambda b, c, be, nr: (be[b], 0, nc + cidx(b, c, nr))),
            pl.BlockSpec((None, 1, MOE_TF), lambda b, c, be, nr: (be[b], 0, cidx(b, c, nr))),
            pl.BlockSpec((None, 1, MOE_TF), lambda b, c, be, nr: (be[b], 0, nc + cidx(b, c, nr))),
            pl.BlockSpec((None, MOE_TF, d), lambda b, c, be, nr: (be[b], cidx(b, c, nr), 0)),
            pl.BlockSpec((None, 1, d), lambda b, c, be, nr: (be[b], 0, 0)),
        ],
        out_specs=pl.BlockSpec(memory_space=pl.ANY),
        scratch_shapes=[
            pltpu.VMEM((MOE_TM, half), U32),
            pltpu.VMEM((MOE_TM, d), F32),
            pltpu.VMEM((d, MOE_TF), BF16),
            pltpu.VMEM((d, MOE_TF), BF16),
            pltpu.VMEM((MOE_TF, d), BF16),
            pltpu.SemaphoreType.DMA,
            pltpu.SemaphoreType.DMA,
        ],
    )
    return pl.pallas_call(
        functools.partial(_moe_kernel, nc=nc),
        grid_spec=grid_spec,
        out_shape=jax.ShapeDtypeStruct((n_tok * TOP_K, d), F32),
        compiler_params=_cparams(("arbitrary", "arbitrary")),
        name="moe",
    )(be, nrows, tok_tbl, dst_tbl, gate_tbl, xnp, w_up, w_up,
      b_up.reshape(n_exp, 1, ff2), b_up.reshape(n_exp, 1, ff2), w_down, b_down.reshape(n_exp, 1, d))


def _moe_schedule(logits, n_exp):
    n = logits.shape[0]
    top_v, top_i = lax.top_k(logits, TOP_K)
    gates = jax.nn.softmax(top_v, axis=-1)
    na = n * TOP_K
    flat_e = top_i.reshape(-1).astype(jnp.int32)
    flat_g = gates.reshape(-1)
    order = jnp.argsort(flat_e, stable=True).astype(jnp.int32)
    counts = jnp.bincount(flat_e, length=n_exp).astype(jnp.int32)
    starts = jnp.cumsum(counts) - counts
    nblk = (counts + MOE_TM - 1) // MOE_TM
    bsz = -(-counts // jnp.maximum(nblk, 1))
    bsz = (bsz + 7) // 8 * 8
    bends = jnp.cumsum(nblk)
    nbmax = -(-na // MOE_TM) + n_exp
    bid = jnp.arange(nbmax, dtype=jnp.int32)
    used = bid < bends[-1]
    e_b = jnp.minimum(jnp.searchsorted(bends, jnp.minimum(bid, bends[-1] - 1), side="right"),
                      n_exp - 1).astype(jnp.int32)
    jb = bid - (bends[e_b] - nblk[e_b])
    row0 = starts[e_b] + jb * bsz[e_b]
    nrows = jnp.where(used, jnp.clip(counts[e_b] - jb * bsz[e_b], 0, bsz[e_b]), 0).astype(jnp.int32)
    pos = row0[:, None] + jnp.arange(MOE_TM, dtype=jnp.int32)[None, :]
    inb = jnp.arange(MOE_TM, dtype=jnp.int32)[None, :] < nrows[:, None]
    a = order[jnp.clip(pos, 0, na - 1)]
    tok_tbl = jnp.where(inb, a // TOP_K, 0).astype(jnp.int32).reshape(nbmax, 1, MOE_TM)
    dst_tbl = jnp.where(inb, a, 0).astype(jnp.int32).reshape(nbmax, 1, MOE_TM)
    gate_tbl = jnp.where(inb, flat_g[a], 0.0).astype(F32).reshape(nbmax, MOE_TM, 1)
    return e_b, nrows, tok_tbl, dst_tbl, gate_tbl


def _final_kernel(x2_ref, y4_ref, g_ref, o_ref):
    d = x2_ref.shape[1]
    y = y4_ref[:, 0:d]
    for k in range(1, TOP_K):
        y = y + y4_ref[:, k * d:(k + 1) * d]
    x = x2_ref[...] + y
    ms = jnp.mean(x * x, axis=-1, keepdims=True)
    o_ref[...] = x * lax.rsqrt(ms + EPS) * g_ref[...]


def _final(x2, y4, g, *, tm, row0, n):
    d = x2.shape[1]
    assert n % tm == 0 and row0 % tm == 0
    off = row0 // tm
    return pl.pallas_call(
        _final_kernel, grid=(n // tm,),
        in_specs=[pl.BlockSpec((tm, d), lambda i: (i + off, 0)),
                  pl.BlockSpec((tm, TOP_K * d), lambda i: (i + off, 0)),
                  pl.BlockSpec((1, d), lambda i: (0, 0))],
        out_specs=pl.BlockSpec((tm, d), lambda i: (i, 0)),
        out_shape=jax.ShapeDtypeStruct((n, d), F32),
        compiler_params=_cparams(("arbitrary",)),
        name="final",
    )(x2, y4, g)


def kernel(x_prompt, x_sample, state_b0_kv, state_b1_kv, state_b2_kv, norm_mix_g, w_in, ln_va_g, ln_va_b,
           w_spatial, b_spatial, norm_out_b_g, norm_out_a_g, w_out, norm_ffn_g, w_router, b_router,
           w_up, b_up, w_down, b_down, norm_final_g):
    bp, sp, d = x_prompt.shape
    bs, ss, _ = x_sample.shape
    depth = w_in.shape[0]
    assert depth == 1 and ss == 1
    hpg = state_b0_kv.shape[4]
    gw = hpg * HEAD_DIM
    attn_w = N_GROUPS * gw
    sgu_w = ln_va_g.shape[1]
    n_exp = w_router.shape[2]
    past_len = 8192
    npr, nsm = bp * sp, bs * ss
    n_total = npr + nsm
    states = (state_b0_kv[0], state_b1_kv[0], state_b2_kv[0])

    w_in_b = w_in[0].astype(BF16)
    w_out_b = w_out[0].astype(BF16)
    g_mix = norm_mix_g[0].reshape(1, d)
    ln_g, ln_b = ln_va_g[0].reshape(1, sgu_w), ln_va_b[0].reshape(1, sgu_w)
    g_b = norm_out_b_g[0].reshape(1, attn_w)
    g_a = norm_out_a_g[0].reshape(1, sgu_w)
    g_ffn = norm_ffn_g[0].reshape(1, d)
    ng = w_spatial.shape[1]
    cw = sgu_w // ng
    bias_tc = jnp.repeat(b_spatial[0].T, cw, axis=1)
    w00 = jnp.repeat(w_spatial[0][:, 0, 0], cw).reshape(1, sgu_w)
    b00 = jnp.repeat(b_spatial[0][:, 0], cw).reshape(1, sgu_w)
    wr = jnp.pad(w_router[0], ((0, 0), (0, LANES - n_exp)))
    wr_hi = wr.astype(BF16)
    wr_hl = jnp.stack([wr_hi, (wr - wr_hi.astype(F32)).astype(BF16)], 0)
    b_r = jnp.pad(b_router[0], (0, LANES - n_exp)).reshape(1, LANES)

    tabs_p = _rope_tables(jnp.tile(jnp.arange(sp), bp))
    tabs_s = _rope_tables(jnp.tile(past_len + jnp.arange(ss), bs))

    xp = x_prompt.reshape(npr, d)
    xs = x_sample.reshape(nsm, d)
    qkv_p, kvf_p, u_p, va_p = _in_proj(xp, g_mix, w_in_b, *tabs_p, ln_g, ln_b, tm=1024, attn_w=attn_w, sgu_w=sgu_w)
    qkv_s, kvf_s, u_s, va_s = _in_proj(xs, g_mix, w_in_b, *tabs_s, ln_g, ln_b, tm=nsm, attn_w=attn_w, sgu_w=sgu_w)

    o_l, lse_l = zip(*[_attn_prompt(qkv_p, g, bp=bp, sp=sp, attn_w=attn_w, gw=gw) for g in range(N_GROUPS)])
    o_p = jnp.concatenate(o_l, axis=1)
    lse_p = jnp.concatenate(lse_l, axis=1)
    o_s, lse_s = _attn_sample(qkv_s, kvf_s, states, attn_w=attn_w, gw=gw)

    sgu_p = _sgu_prompt(u_p, va_p, w_spatial[0], bias_tc, g_a, cps=4)
    sgu_s = _sgu_sample(u_s, va_s, w00, b00, g_a)

    prev = _out_proj(o_p, lse_p, sgu_p, xp, g_b, w_out_b, g_ffn, wr_hl, b_r,
                     tm=256, n_total=n_total, row0=0, gw=gw)
    x2, xnp, logits = _out_proj(o_s.astype(BF16), lse_s, sgu_s, xs, g_b, w_out_b, g_ffn, wr_hl, b_r,
                                tm=nsm, n_total=n_total, row0=npr, prev=prev, gw=gw)

    sched = _moe_schedule(logits[:, :n_exp], n_exp)
    y4 = _moe(xnp, sched, w_up[0], b_up[0], w_down[0], b_down[0]).reshape(n_total, TOP_K * d)
    g_fin = norm_final_g.reshape(1, d)
    y_p = _final(x2, y4, g_fin, tm=512, row0=0, n=npr).reshape(bp, sp, d)
    y_s = _final(x2, y4, g_fin, tm=nsm, row0=npr, n=nsm).reshape(bs, ss, d)

    kp = kvf_p.reshape(bp, sp, 2, N_GROUPS, hpg, HEAD_DIM)
    ksm = kvf_s.reshape(bs, ss, 2, N_GROUPS, hpg, HEAD_DIM)
    kv_prompt = [kp[:, sp - min(w, sp):, :, g][None] for g, w in enumerate(ATTN_WINDOWS)]
    kv_sample = [ksm[:, :, :, g][None] for g in range(N_GROUPS)]
    v_sgu = va_s.reshape(1, bs, ss, sgu_w)
    return (y_p, y_s, *kv_prompt, *kv_sample, v_sgu)
```

```python
import functools

import jax
import jax.numpy as jnp
import numpy as np
from jax import lax
from jax.experimental import pallas as pl
from jax.experimental.pallas import tpu as pltpu

F32 = jnp.float32
BF16 = jnp.bfloat16
U32 = jnp.uint32

HEAD_DIM = 64
ROT_DIM = HEAD_DIM // 4
ROPE_THETA = 500000.0
ATTN_WINDOWS = (128, 512, 2048)
ATTN_DILATIONS = (1, 4, 16)
N_GROUPS = 3
CHUNK = 128
ATTN_BLOCK = 128
TOP_K = 4
SWIGLU_LIMIT = 7.0
SWIGLU_ALPHA = 1.702
EPS = 1e-5
PAST_LEN = 8192

LANES = 128
VMEM_LIMIT = 56 * 1024 * 1024

MOE_SUB = 256
MOE_NSUB = 4
MOE_TM = MOE_NSUB * MOE_SUB
MOE_RING = 2 * MOE_NSUB
MOE_TF = 512


def _cparams(sem):
    return pltpu.CompilerParams(dimension_semantics=sem, vmem_limit_bytes=VMEM_LIMIT)


def _gelu(x):
    return 0.5 * x * (1.0 + lax.erf(x * np.float32(np.sqrt(0.5))))


def _split2(a):
    a1 = a.astype(BF16)
    return a1, (a - a1.astype(F32)).astype(BF16)


def _in_proj_kernel(x_ref, g_ref, w_ref, cos_ref, sa_ref, sb_ref, lng_ref, lnb_ref,
                    qkv_ref, kvf_ref, u_ref, va_ref, xn_sc, *, nq):
    j = pl.program_id(1)

    @pl.when(j == 0)
    def _():
        x = x_ref[...]
        ms = jnp.mean(x * x, axis=-1, keepdims=True)
        xn_sc[...] = (x * lax.rsqrt(ms + EPS) * g_ref[...]).astype(BF16)

    acc = jnp.dot(xn_sc[...], w_ref[...], preferred_element_type=F32)

    @pl.when(j < 2 * nq)
    def _():
        cos, sa, sb = cos_ref[...], sa_ref[...], sb_ref[...]
        parts = []
        for s in range(acc.shape[1] // LANES):
            z = acc[:, s * LANES:(s + 1) * LANES]
            parts.append(z * cos + pltpu.roll(z, ROT_DIM // 2, 1) * sa
                         + pltpu.roll(z, LANES - ROT_DIM // 2, 1) * sb)
        r = jnp.concatenate(parts, axis=1)
        qkv_ref[...] = r.astype(BF16)

        @pl.when(j >= nq)
        def _():
            kvf_ref[...] = r

    @pl.when((j >= 2 * nq) & (j < 3 * nq))
    def _():
        qkv_ref[...] = acc.astype(BF16)
        kvf_ref[...] = acc

    @pl.when(j == 3 * nq)
    def _():
        u_ref[...] = _gelu(acc).astype(BF16)

    @pl.when(j == 3 * nq + 1)
    def _():
        t = _gelu(acc)
        mu = jnp.mean(t, axis=-1, keepdims=True)
        var = jnp.mean(jnp.square(t - mu), axis=-1, keepdims=True)
        va_ref[...] = (t - mu) * lax.rsqrt(var + EPS) * lng_ref[...] + lnb_ref[...]


def _in_proj(x, g, w_bf16, cos_t, sa_t, sb_t, ln_g, ln_b, *, tm, attn_w, sgu_w):
    n, d = x.shape
    bw = sgu_w
    nq = attn_w // bw
    nj = 3 * nq + 2
    assert w_bf16.shape[1] == nj * bw
    row = lambda i, j: (i, 0)
    return pl.pallas_call(
        functools.partial(_in_proj_kernel, nq=nq),
        grid=(pl.cdiv(n, tm), nj),
        in_specs=[
            pl.BlockSpec((tm, d), row),
            pl.BlockSpec((1, d), lambda i, j: (0, 0)),
            pl.BlockSpec((d, bw), lambda i, j: (0, j)),
            pl.BlockSpec((tm, LANES), row),
            pl.BlockSpec((tm, LANES), row),
            pl.BlockSpec((tm, LANES), row),
            pl.BlockSpec((1, bw), lambda i, j: (0, 0)),
            pl.BlockSpec((1, bw), lambda i, j: (0, 0)),
        ],
        out_specs=[
            pl.BlockSpec((tm, bw), lambda i, j: (i, jnp.minimum(j, 3 * nq - 1))),
            pl.BlockSpec((tm, bw), lambda i, j: (i, jnp.clip(j - nq, 0, 2 * nq - 1))),
            pl.BlockSpec((tm, bw), row),
            pl.BlockSpec((tm, bw), row),
        ],
        out_shape=[
            jax.ShapeDtypeStruct((n, 3 * attn_w), BF16),
            jax.ShapeDtypeStruct((n, 2 * attn_w), F32),
            jax.ShapeDtypeStruct((n, sgu_w), BF16),
            jax.ShapeDtypeStruct((n, sgu_w), F32),
        ],
        scratch_shapes=[pltpu.VMEM((tm, d), BF16)],
        compiler_params=_cparams(("arbitrary", "arbitrary")),
        name="in_proj",
    )(x, g, w_bf16, cos_t, sa_t, sb_t, ln_g, ln_b)


def _rope_tables(pos):
    half = ROT_DIM // 2
    inv = ROPE_THETA ** (-jnp.arange(0, ROT_DIM, 2, dtype=F32) / ROT_DIM)
    ang = pos.astype(F32)[:, None] * inv[None, :]
    cos, sin = jnp.cos(ang), jnp.sin(ang)
    rows = pos.shape[0]
    one = jnp.ones((rows, HEAD_DIM - ROT_DIM), F32)
    zero = jnp.zeros((rows, HEAD_DIM - ROT_DIM), F32)
    zh = jnp.zeros((rows, half), F32)
    cos_h = jnp.concatenate([cos, cos, one], axis=1)
    sa_h = jnp.concatenate([zh, sin, zero], axis=1)
    sb_h = jnp.concatenate([-sin, zh, zero], axis=1)
    rep = LANES // HEAD_DIM
    return tuple(jnp.tile(t, (1, rep)) for t in (cos_h, sa_h, sb_h))


def _attn_prompt_kernel(q_ref, kp_ref, kc_ref, vp_ref, vc_ref, o_ref, lse_ref, *, nk, hpg):
    n = pl.program_id(2)
    q = q_ref[...]
    k = jnp.concatenate([kp_ref[...], kc_ref[...]], axis=0)
    v = jnp.concatenate([vp_ref[...], vc_ref[...]], axis=0)
    tq, tk = q.shape[0], k.shape[0]
    qi = lax.broadcasted_iota(jnp.int32, (tq, tk), 0)
    kj = lax.broadcasted_iota(jnp.int32, (tq, tk), 1)
    dist = qi + tq - kj
    valid = (dist >= 0) & (dist <= nk) & ((n - 1) * tq + kj >= 0)
    outs, lses = [], []
    for h in range(hpg):
        sl = slice(h * HEAD_DIM, (h + 1) * HEAD_DIM)
        s = lax.dot_general(q[:, sl], k[:, sl], (((1,), (1,)), ((), ())),
                            preferred_element_type=F32) * (HEAD_DIM ** -0.5)
        s = jnp.where(valid, s, -jnp.inf)
        m = jnp.max(s, axis=-1, keepdims=True)
        p = jnp.exp(s - m)
        den = jnp.sum(p, axis=-1, keepdims=True)
        o = jnp.dot(p.astype(BF16), v[:, sl], preferred_element_type=F32) / den
        outs.append(o)
        lses.append(jnp.broadcast_to(m + jnp.log(den), (tq, HEAD_DIM)))
    o_ref[...] = jnp.concatenate(outs, axis=1).astype(BF16)
    lse_ref[...] = jnp.concatenate(lses, axis=1)


def _attn_prompt(qkv, g, *, bp, sp, attn_w, gw):
    dil, win = ATTN_DILATIONS[g], ATTN_WINDOWS[g]
    nk = win // dil
    assert nk <= ATTN_BLOCK and sp % (dil * ATTN_BLOCK) == 0
    L = sp // dil
    nb = L // ATTN_BLOCK
    ncol = 3 * attn_w // gw
    nq = attn_w // gw
    view = qkv.reshape(bp, L, dil * 3 * attn_w)
    blk = (None, ATTN_BLOCK, gw)

    def spec(col, prev):
        if prev:
            return pl.BlockSpec(blk, lambda b, r, n: (b, jnp.maximum(n - 1, 0), r * ncol + col))
        return pl.BlockSpec(blk, lambda b, r, n: (b, n, r * ncol + col))

    out_spec = pl.BlockSpec(blk, lambda b, r, n: (b, n, r))
    o, lse = pl.pallas_call(
        functools.partial(_attn_prompt_kernel, nk=nk, hpg=gw // HEAD_DIM),
        grid=(bp, dil, nb),
        in_specs=[spec(g, False), spec(nq + g, True), spec(nq + g, False),
                  spec(2 * nq + g, True), spec(2 * nq + g, False)],
        out_specs=[out_spec, out_spec],
        out_shape=[jax.ShapeDtypeStruct((bp, L, dil * gw), BF16),
                   jax.ShapeDtypeStruct((bp, L, dil * gw), F32)],
        compiler_params=_cparams(("arbitrary", "arbitrary", "arbitrary")),
        name=f"attn_prompt_g{g}",
    )(view, view, view, view, view)
    return o.reshape(bp * sp, gw), lse.reshape(bp * sp, gw)


def _attn_sample_kernel(q_ref, kvn_ref, s0_ref, s1_ref, s2_ref, o_ref, lse_ref, *, attn_w, gw):
    rows = 16
    ri = lax.broadcasted_iota(jnp.int32, (rows, gw), 0)
    lh = lax.broadcasted_iota(jnp.int32, (rows, gw), 1) // HEAD_DIM
    diag = ri == lh
    scale = HEAD_DIM ** -0.5
    q_all = q_ref[...].astype(F32)
    kvn = kvn_ref[...]
    nt = (((1,), (1,)), ((), ()))
    d = functools.partial(jnp.dot, preferred_element_type=F32)
    dn = functools.partial(lax.dot_general, dimension_numbers=nt, preferred_element_type=F32)
    o_parts, lse_parts = [], []
    for g, s_ref in enumerate((s0_ref, s1_ref, s2_ref)):
        dil = ATTN_DILATIONS[g]
        nk = ATTN_WINDOWS[g] // dil
        lb = s_ref.shape[1]
        q = q_all[:, g * gw:(g + 1) * gw]
        k_new = kvn[:, g * gw:(g + 1) * gw]
        v_new = kvn[:, attn_w + g * gw:attn_w + (g + 1) * gw]
        qblk = jnp.where(diag, jnp.broadcast_to(q, (rows, gw)), 0.0)
        q1, q2 = _split2(qblk)
        k1, k2 = _split2(s_ref[:gw, :])
        s = (d(q1, k1) + d(q2, k1) + d(q1, k2)) * scale
        back = lb - lax.broadcasted_iota(jnp.int32, (rows, lb), 1)
        valid = ((back & (dil - 1)) == 0) & (back <= dil * nk)
        s = jnp.where(valid, s, -jnp.inf)
        s_new = jnp.sum(qblk * k_new, axis=1, keepdims=True) * scale
        m = jnp.maximum(jnp.max(s, axis=1, keepdims=True), s_new)
        p = jnp.exp(s - m)
        p_new = jnp.exp(s_new - m)
        den = jnp.sum(p, axis=1, keepdims=True) + p_new
        lse = m + jnp.log(den)
        p1, p2 = _split2(p)
        v1, v2 = _split2(s_ref[gw:, :])
        o_full = (dn(p1, v1) + dn(p2, v1) + dn(p1, v2) + p_new * v_new) / den
        o_parts.append(jnp.sum(jnp.where(diag, o_full, 0.0), axis=0, keepdims=True))
        lse_parts.append(jnp.sum(jnp.where(diag, jnp.broadcast_to(lse, (rows, gw)), 0.0), axis=0, keepdims=True))
    o_ref[...] = jnp.concatenate(o_parts, axis=1)
    lse_ref[...] = jnp.concatenate(lse_parts, axis=1)


def _attn_sample(qkv_s, kvf_s, states, *, attn_w, gw):
    bs = qkv_s.shape[0]
    views, specs = [], []
    for g, st in enumerate(states):
        dil, win = ATTN_DILATIONS[g], ATTN_WINDOWS[g]
        lb = st.shape[1]
        assert lb == win and (dil & (dil - 1)) == 0 and st.shape[0] == bs
        views.append(jnp.transpose(st, (0, 2, 3, 4, 1)).reshape(bs, 2 * gw, lb))
        specs.append(pl.BlockSpec((None, 2 * gw, lb), lambda b: (b, 0, 0)))
    row3 = lambda w: pl.BlockSpec((None, 1, w), lambda b: (b, 0, 0))
    o, lse = pl.pallas_call(
        functools.partial(_attn_sample_kernel, attn_w=attn_w, gw=gw),
        grid=(bs,),
        in_specs=[row3(3 * attn_w), row3(2 * attn_w), *specs],
        out_specs=[row3(attn_w), row3(attn_w)],
        out_shape=[jax.ShapeDtypeStruct((bs, 1, attn_w), F32),
                   jax.ShapeDtypeStruct((bs, 1, attn_w), F32)],
        compiler_params=_cparams(("arbitrary",)),
        name="attn_sample",
    )(qkv_s.reshape(bs, 1, 3 * attn_w), kvf_s.reshape(bs, 1, 2 * attn_w), *views)
    return o.reshape(bs, attn_w), lse.reshape(bs, attn_w)


def _sgu_kernel(u_ref, va_ref, w_ref, bias_ref, g_ref, o_ref, *, cps):
    ng = w_ref.shape[0]
    ti = lax.broadcasted_iota(jnp.int32, (CHUNK, CHUNK), 0)
    si = lax.broadcasted_iota(jnp.int32, (CHUNK, CHUNK), 1)
    wt = [jnp.where(ti >= si, w_ref[g], 0.0).astype(BF16) for g in range(ng)]
    cw = va_ref.shape[1] // ng
    for c in range(cps):
        rows = slice(c * CHUNK, (c + 1) * CHUNK)
        va = va_ref[rows, :].astype(BF16)
        f = jnp.concatenate(
            [jnp.dot(wt[g], va[:, g * cw:(g + 1) * cw], preferred_element_type=F32) for g in range(ng)],
            axis=1) + bias_ref[...]
        s = u_ref[rows, :].astype(F32) * f
        ms = jnp.mean(s * s, axis=-1, keepdims=True)
        o_ref[rows, :] = (s * lax.rsqrt(ms + EPS) * g_ref[...]).astype(BF16)


def _sgu_prompt(u, va, w_sp, bias_tc, g_a, *, cps):
    n, w = u.shape
    tm = cps * CHUNK
    assert n % tm == 0
    row = lambda i: (i, 0)
    return pl.pallas_call(
        functools.partial(_sgu_kernel, cps=cps),
        grid=(n // tm,),
        in_specs=[pl.BlockSpec((tm, w), row), pl.BlockSpec((tm, w), row),
                  pl.BlockSpec(w_sp.shape, lambda i: (0, 0, 0)),
                  pl.BlockSpec((CHUNK, w), lambda i: (0, 0)),
                  pl.BlockSpec((1, w), lambda i: (0, 0))],
        out_specs=pl.BlockSpec((tm, w), row),
        out_shape=jax.ShapeDtypeStruct((n, w), BF16),
        compiler_params=_cparams(("arbitrary",)),
        name="sgu_prompt",
    )(u, va, w_sp, bias_tc, g_a)


def _sgu_sample_kernel(u_ref, va_ref, w0_ref, b0_ref, g_ref, o_ref):
    s = u_ref[...].astype(F32) * (va_ref[...] * w0_ref[...] + b0_ref[...])
    ms = jnp.mean(s * s, axis=-1, keepdims=True)
    o_ref[...] = (s * lax.rsqrt(ms + EPS) * g_ref[...]).astype(BF16)


def _sgu_sample(u, va, w0, b0, g_a):
    n, w = u.shape
    full = pl.BlockSpec((n, w), lambda i: (0, 0))
    vec = pl.BlockSpec((1, w), lambda i: (0, 0))
    return pl.pallas_call(
        _sgu_sample_kernel, grid=(1,),
        in_specs=[full, full, vec, vec, vec], out_specs=full,
        out_shape=jax.ShapeDtypeStruct((n, w), BF16),
        name="sgu_sample",
    )(u, va, w0, b0, g_a)


def _out_proj_kernel(*refs, gw, aliased):
    if aliased:
        refs = refs[:9] + refs[12:]
    (o_ref, lse_ref, sgu_ref, x_ref, gb_ref, wout_ref, gffn_ref, wr_ref, br_ref,
     x2_ref, xnp_ref, lg_ref) = refs
    lse = lse_ref[...]
    ls = [lse[:, g * gw:(g + 1) * gw] for g in range(N_GROUPS)]
    m = jnp.maximum(jnp.maximum(ls[0], ls[1]), ls[2])
    es = [jnp.exp(l - m) for l in ls]
    tot = es[0] + es[1] + es[2]
    o = o_ref[...].astype(F32)
    attn = jnp.concatenate([o[:, g * gw:(g + 1) * gw] * (es[g] / tot) for g in range(N_GROUPS)], axis=1)
    ms = jnp.mean(attn * attn, axis=-1, keepdims=True)
    attn_n = (attn * lax.rsqrt(ms + EPS) * gb_ref[...]).astype(BF16)
    aw = attn_n.shape[1]
    y = jnp.dot(attn_n, wout_ref[:aw, :], preferred_element_type=F32)
    y = y + jnp.dot(sgu_ref[...], wout_ref[aw:, :], preferred_element_type=F32)
    x2 = x_ref[...] + y
    x2_ref[...] = x2
    ms2 = jnp.mean(x2 * x2, axis=-1, keepdims=True)
    xn = x2 * lax.rsqrt(ms2 + EPS) * gffn_ref[...]
    x1 = xn.astype(BF16)
    xr = (xn - x1.astype(F32)).astype(BF16)
    d = functools.partial(jnp.dot, preferred_element_type=F32)
    lg_ref[...] = d(x1, wr_ref[0]) + d(x1, wr_ref[1]) + d(xr, wr_ref[0]) + br_ref[...]
    bits = lax.bitcast_convert_type(x1.astype(F32), U32)
    half = bits.shape[1] // 2
    xnp_ref[...] = bits[:, :half] | (bits[:, half:] >> 16)


def _out_proj(o, lse, sgu_n, x, g_b, wout_bf16, g_ffn, wr_hl, b_r, *, tm, n_total, row0, prev=None, gw):
    n, d = x.shape
    aw = o.shape[1]
    assert n % tm == 0 and row0 % tm == 0
    off = row0 // tm
    row = lambda i: (i, 0)
    orow = lambda i: (i + off, 0)
    const2 = lambda i: (0, 0)
    in_specs = [
        pl.BlockSpec((tm, aw), row), pl.BlockSpec((tm, aw), row), pl.BlockSpec((tm, d - aw), row),
        pl.BlockSpec((tm, d), row), pl.BlockSpec((1, aw), const2),
        pl.BlockSpec((d, d), const2), pl.BlockSpec((1, d), const2),
        pl.BlockSpec((2, d, LANES), lambda i: (0, 0, 0)), pl.BlockSpec((1, LANES), const2),
    ]
    args = [o, lse, sgu_n, x, g_b, wout_bf16, g_ffn, wr_hl, b_r]
    aliases = {}
    if prev is not None:
        in_specs += [pl.BlockSpec(memory_space=pl.ANY)] * 3
        args += list(prev)
        aliases = {9: 0, 10: 1, 11: 2}
    return pl.pallas_call(
        functools.partial(_out_proj_kernel, gw=gw, aliased=prev is not None),
        grid=(n // tm,),
        in_specs=in_specs,
        out_specs=[pl.BlockSpec((tm, d), orow), pl.BlockSpec((tm, d // 2), orow),
                   pl.BlockSpec((tm, LANES), orow)],
        out_shape=[jax.ShapeDtypeStruct((n_total, d), F32),
                   jax.ShapeDtypeStruct((n_total, d // 2), U32),
                   jax.ShapeDtypeStruct((n_total, LANES), F32)],
        input_output_aliases=aliases,
        compiler_params=_cparams(("arbitrary",)),
        name="out_proj",
    )(*args)


def _moe_kernel(be_ref, ns_ref, sb_ref, pro_ref, pref_ref, dst_ref, gate_ref, xn_hbm,
                wg_ref, wl_ref, bg_ref, bl_ref, wd_ref, bd_ref, out_hbm,
                xb_sc, acc_sc, wg_sc, wl_sc, wd_sc, sem_in, sem_out, *, nc):
    b = pl.program_id(0)
    c = pl.program_id(1)
    nb = pl.num_programs(0)
    ns = ns_ref[b]
    s0 = sb_ref[b]
    del be_ref
    ring_rows = MOE_RING * MOE_SUB
    per_it = MOE_SUB // nc

    def row_in(tok, ring_row, slot):
        return pltpu.make_async_copy(xn_hbm.at[pl.ds(tok, 1), :], xb_sc.at[pl.ds(ring_row, 1), :], sem_in.at[slot])

    def sub_in_wait(slot):
        pltpu.make_async_copy(xn_hbm.at[pl.ds(0, MOE_SUB), :], xb_sc.at[pl.ds(0, MOE_SUB), :],
                              sem_in.at[slot]).wait()

    def sub_out_wait():
        pltpu.make_async_copy(acc_sc.at[pl.ds(0, MOE_SUB), :], out_hbm.at[pl.ds(0, MOE_SUB), :], sem_out).wait()

    @pl.when((b == 0) & (c == 0))
    def _():
        def start(i, carry):
            row_in(pro_ref[0, i], i, i // MOE_SUB).start()
            return carry
        lax.fori_loop(0, MOE_TM, start, 0)

    @pl.when(ns > 0)
    def _():
        @pl.when(c == 0)
        def _():
            for s in range(MOE_NSUB):
                @pl.when(s < ns)
                def _():
                    sub_in_wait((s0 + s) % MOE_RING)

        wg_sc[...] = wg_ref[...].astype(BF16)
        wl_sc[...] = wl_ref[...].astype(BF16)
        wd_sc[...] = wd_ref[...].astype(BF16)
        half = xb_sc.shape[1]

        def sub(s, carry):
            x0 = pl.multiple_of(((s0 + s) % MOE_RING) * MOE_SUB, MOE_SUB)
            a0 = pl.multiple_of(s * MOE_SUB, MOE_SUB)
            rows = pl.ds(a0, MOE_SUB)
            i0 = (c * ns + s) * per_it
            p0 = (s0 + MOE_NSUB) * MOE_SUB + i0
            slot = (p0 // MOE_SUB) % MOE_RING
            q0 = p0 % ring_rows
            for k in range(per_it):
                row_in(pref_ref[0, i0 + k], q0 + k, slot).start()

            w = xb_sc[pl.ds(x0, MOE_SUB), :]
            x_hi = lax.bitcast_convert_type(w & jnp.uint32(0xFFFF0000), F32).astype(BF16)
            x_lo = lax.bitcast_convert_type(w << 16, F32).astype(BF16)
            d = functools.partial(jnp.dot, preferred_element_type=F32)
            hg = d(x_hi, wg_sc[:half, :]) + d(x_lo, wg_sc[half:, :]) + bg_ref[...]
            hl = d(x_hi, wl_sc[:half, :]) + d(x_lo, wl_sc[half:, :]) + bl_ref[...]
            glu = jnp.minimum(hg, SWIGLU_LIMIT)
            lin = jnp.clip(hl, -SWIGLU_LIMIT, SWIGLU_LIMIT)
            a = glu * jax.nn.sigmoid(SWIGLU_ALPHA * glu) * (lin + 1.0)
            y = d(a.astype(BF16), wd_sc[...])

            @pl.when(c == 0)
            def _():
                acc_sc[rows, :] = y + bd_ref[...]

            @pl.when((c > 0) & (c < nc - 1))
            def _():
                acc_sc[rows, :] += y

            @pl.when(c == nc - 1)
            def _():
                acc_sc[rows, :] = (acc_sc[rows, :] + y) * gate_ref[rows, :]
                for k in range(MOE_SUB):
                    pltpu.make_async_copy(acc_sc.at[pl.ds(a0 + k, 1), :],
                                          out_hbm.at[pl.ds(dst_ref[0, a0 + k], 1), :], sem_out).start()
            return carry

        lax.fori_loop(0, ns, sub, 0)

        @pl.when(c == nc - 1)
        def _():
            def wait(s, carry):
                sub_out_wait()
                return carry
            lax.fori_loop(0, ns, wait, 0)

    @pl.when((b == nb - 1) & (c == nc - 1))
    def _():
        total = sb_ref[nb - 1] + ns_ref[nb - 1]
        for k in range(MOE_NSUB):
            sub_in_wait((total + k) % MOE_RING)


def _moe(xnp, sched, w_up, b_up, w_down, b_down, *, n_rows_out):
    be, nsub, sub0, pro_tbl, pref_tbl, dst_tbl, gate_tbl = sched
    n_tok, half = xnp.shape
    d = 2 * half
    n_exp, _, ff2 = w_up.shape
    ff = ff2 // 2
    assert ff % MOE_TF == 0 and MOE_TF % LANES == 0
    nc = ff // MOE_TF
    assert MOE_SUB % nc == 0 and (MOE_RING * MOE_SUB) % (MOE_SUB // nc) == 0
    nbmax = be.shape[0]

    def cidx(b, c, ns):
        return jnp.where(ns[b] > 0, c, nc - 1)

    smem_tbl = lambda: pl.BlockSpec((None, 1, MOE_TM), lambda b, c, be, ns, sb: (b, 0, 0), memory_space=pltpu.SMEM)
    grid_spec = pltpu.PrefetchScalarGridSpec(
        num_scalar_prefetch=3,
        grid=(nbmax, nc),
        in_specs=[
            pl.BlockSpec((1, MOE_TM), lambda b, c, be, ns, sb: (0, 0), memory_space=pltpu.SMEM),
            smem_tbl(),
            smem_tbl(),
            pl.BlockSpec((None, MOE_TM, 1), lambda b, c, be, ns, sb: (b, 0, 0)),
            pl.BlockSpec(memory_space=pl.ANY),
            pl.BlockSpec((None, d, MOE_TF), lambda b, c, be, ns, sb: (be[b], 0, cidx(b, c, ns))),
            pl.BlockSpec((None, d, MOE_TF), lambda b, c, be, ns, sb: (be[b], 0, nc + cidx(b, c, ns))),
            pl.BlockSpec((None, 1, MOE_TF), lambda b, c, be, ns, sb: (be[b], 0, cidx(b, c, ns))),
            pl.BlockSpec((None, 1, MOE_TF), lambda b, c, be, ns, sb: (be[b], 0, nc + cidx(b, c, ns))),
            pl.BlockSpec((None, MOE_TF, d), lambda b, c, be, ns, sb: (be[b], cidx(b, c, ns), 0)),
            pl.BlockSpec((None, 1, d), lambda b, c, be, ns, sb: (be[b], 0, 0)),
        ],
        out_specs=pl.BlockSpec(memory_space=pl.ANY),
        scratch_shapes=[
            pltpu.VMEM((MOE_RING * MOE_SUB, half), U32),
            pltpu.VMEM((MOE_TM, d), F32),
            pltpu.VMEM((d, MOE_TF), BF16),
            pltpu.VMEM((d, MOE_TF), BF16),
            pltpu.VMEM((MOE_TF, d), BF16),
            pltpu.SemaphoreType.DMA((MOE_RING,)),
            pltpu.SemaphoreType.DMA,
        ],
    )
    return pl.pallas_call(
        functools.partial(_moe_kernel, nc=nc),
        grid_spec=grid_spec,
        out_shape=jax.ShapeDtypeStruct((n_rows_out, d), F32),
        compiler_params=_cparams(("arbitrary", "arbitrary")),
        name="moe",
    )(be, nsub, sub0, pro_tbl, pref_tbl, dst_tbl, gate_tbl, xnp, w_up, w_up,
      b_up.reshape(n_exp, 1, ff2), b_up.reshape(n_exp, 1, ff2), w_down, b_down.reshape(n_exp, 1, d))


def _moe_schedule(logits, n_exp, n_pad):
    n = logits.shape[0]
    i32 = jnp.int32
    top_v, top_i = lax.top_k(logits, TOP_K)
    gates = jax.nn.softmax(top_v, axis=-1)
    na = n * TOP_K
    flat_e = top_i.reshape(-1).astype(i32)
    flat_g = gates.reshape(-1)
    order = jnp.argsort(flat_e, stable=True).astype(i32)
    counts = jnp.bincount(flat_e, length=n_exp).astype(i32)
    starts = jnp.cumsum(counts) - counts
    ns_e = (counts + MOE_SUB - 1) // MOE_SUB
    nblk_e = (ns_e + MOE_NSUB - 1) // MOE_NSUB
    spb_e = -(-ns_e // jnp.maximum(nblk_e, 1))
    sub_ends = jnp.cumsum(ns_e)
    sub_starts = sub_ends - ns_e
    total_subs = sub_ends[-1]
    bends = jnp.cumsum(nblk_e)
    total_blocks = bends[-1]

    nsub_max = -(-na // MOE_SUB) + n_exp
    nbmax = -(-nsub_max // MOE_NSUB) + n_exp
    bid = jnp.arange(nbmax, dtype=i32)
    used = bid < total_blocks
    e_b = jnp.minimum(jnp.searchsorted(bends, jnp.minimum(bid, total_blocks - 1), side="right"),
                      n_exp - 1).astype(i32)
    jb = bid - (bends[e_b] - nblk_e[e_b])
    sub0 = jnp.where(used, sub_starts[e_b] + jb * spb_e[e_b], total_subs).astype(i32)
    nsub = jnp.where(used, jnp.clip(ns_e[e_b] - jb * spb_e[e_b], 0, spb_e[e_b]), 0).astype(i32)

    gs = nsub_max + 2 * MOE_NSUB
    gid = jnp.arange(gs, dtype=i32)
    e_g = jnp.minimum(jnp.searchsorted(sub_ends, gid, side="right"), n_exp - 1).astype(i32)
    lane = jnp.arange(MOE_SUB, dtype=i32)[None, :]
    within = (gid - sub_starts[e_g])[:, None] * MOE_SUB + lane
    valid = (gid < total_subs)[:, None] & (within < counts[e_g][:, None])
    a = order[jnp.clip(starts[e_g][:, None] + within, 0, na - 1)]
    tok_f = jnp.where(valid, a // TOP_K, 0).astype(i32).reshape(-1)
    dst_f = jnp.where(valid, (a % TOP_K) * n_pad + a // TOP_K, n + lane).astype(i32).reshape(-1)
    gate_f = jnp.where(valid, flat_g[a], 0.0).astype(F32).reshape(-1)

    own = sub0[:, None] * MOE_SUB + jnp.arange(MOE_TM, dtype=i32)[None, :]
    dst_tbl = dst_f[own].reshape(nbmax, 1, MOE_TM)
    gate_tbl = gate_f[own].reshape(nbmax, MOE_TM, 1)
    pref_tbl = tok_f[own + MOE_TM].reshape(nbmax, 1, MOE_TM)
    pro_tbl = tok_f[:MOE_TM].reshape(1, MOE_TM)
    return e_b, nsub, sub0, pro_tbl, pref_tbl, dst_tbl, gate_tbl


def _final_kernel(x2_ref, y4_ref, g_ref, o_ref):
    y = y4_ref[0]
    for k in range(1, TOP_K):
        y = y + y4_ref[k]
    x = x2_ref[...] + y
    ms = jnp.mean(x * x, axis=-1, keepdims=True)
    o_ref[...] = x * lax.rsqrt(ms + EPS) * g_ref[...]


def _final(x2, y4, g, *, tm, row0, n):
    d = x2.shape[1]
    assert n % tm == 0 and row0 % tm == 0
    off = row0 // tm
    return pl.pallas_call(
        _final_kernel, grid=(n // tm,),
        in_specs=[pl.BlockSpec((tm, d), lambda i: (i + off, 0)),
                  pl.BlockSpec((TOP_K, tm, d), lambda i: (0, i + off, 0)),
                  pl.BlockSpec((1, d), lambda i: (0, 0))],
        out_specs=pl.BlockSpec((tm, d), lambda i: (i, 0)),
        out_shape=jax.ShapeDtypeStruct((n, d), F32),
        compiler_params=_cparams(("arbitrary",)),
        name="final",
    )(x2, y4, g)


def kernel(x_prompt, x_sample, state_b0_kv, state_b1_kv, state_b2_kv, norm_mix_g, w_in, ln_va_g, ln_va_b,
           w_spatial, b_spatial, norm_out_b_g, norm_out_a_g, w_out, norm_ffn_g, w_router, b_router,
           w_up, b_up, w_down, b_down, norm_final_g):
    bp, sp, d = x_prompt.shape
    bs, ss, _ = x_sample.shape
    depth = w_in.shape[0]
    assert depth == 1 and ss == 1
    hpg = state_b0_kv.shape[4]
    gw = hpg * HEAD_DIM
    attn_w = N_GROUPS * gw
    sgu_w = ln_va_g.shape[1]
    n_exp = w_router.shape[2]
    npr, nsm = bp * sp, bs * ss
    n_total = npr + nsm
    states = (state_b0_kv[0], state_b1_kv[0], state_b2_kv[0])

    w_in_b = w_in[0].astype(BF16)
    w_out_b = w_out[0].astype(BF16)
    g_mix = norm_mix_g[0].reshape(1, d)
    ln_g, ln_b = ln_va_g[0].reshape(1, sgu_w), ln_va_b[0].reshape(1, sgu_w)
    g_b = norm_out_b_g[0].reshape(1, attn_w)
    g_a = norm_out_a_g[0].reshape(1, sgu_w)
    g_ffn = norm_ffn_g[0].reshape(1, d)
    ng = w_spatial.shape[1]
    cw = sgu_w // ng
    bias_tc = jnp.repeat(b_spatial[0].T, cw, axis=1)
    w00 = jnp.repeat(w_spatial[0][:, 0, 0], cw).reshape(1, sgu_w)
    b00 = jnp.repeat(b_spatial[0][:, 0], cw).reshape(1, sgu_w)
    wr = jnp.pad(w_router[0], ((0, 0), (0, LANES - n_exp)))
    wr_hi = wr.astype(BF16)
    wr_hl = jnp.stack([wr_hi, (wr - wr_hi.astype(F32)).astype(BF16)], 0)
    b_r = jnp.pad(b_router[0], (0, LANES - n_exp)).reshape(1, LANES)

    tabs_p = _rope_tables(jnp.tile(jnp.arange(sp), bp))
    tabs_s = _rope_tables(jnp.tile(PAST_LEN + jnp.arange(ss), bs))

    xp = x_prompt.reshape(npr, d)
    xs = x_sample.reshape(nsm, d)
    qkv_p, kvf_p, u_p, va_p = _in_proj(xp, g_mix, w_in_b, *tabs_p, ln_g, ln_b, tm=1024, attn_w=attn_w, sgu_w=sgu_w)
    qkv_s, kvf_s, u_s, va_s = _in_proj(xs, g_mix, w_in_b, *tabs_s, ln_g, ln_b, tm=nsm, attn_w=attn_w, sgu_w=sgu_w)

    o_l, lse_l = zip(*[_attn_prompt(qkv_p, g, bp=bp, sp=sp, attn_w=attn_w, gw=gw) for g in range(N_GROUPS)])
    o_p = jnp.concatenate(o_l, axis=1)
    lse_p = jnp.concatenate(lse_l, axis=1)
    o_s, lse_s = _attn_sample(qkv_s, kvf_s, states, attn_w=attn_w, gw=gw)

    sgu_p = _sgu_prompt(u_p, va_p, w_spatial[0], bias_tc, g_a, cps=4)
    sgu_s = _sgu_sample(u_s, va_s, w00, b00, g_a)

    prev = _out_proj(o_p, lse_p, sgu_p, xp, g_b, w_out_b, g_ffn, wr_hl, b_r,
                     tm=256, n_total=n_total, row0=0, gw=gw)
    x2, xnp, logits = _out_proj(o_s.astype(BF16), lse_s, sgu_s, xs, g_b, w_out_b, g_ffn, wr_hl, b_r,
                                tm=nsm, n_total=n_total, row0=npr, prev=prev, gw=gw)

    n_pad = n_total + MOE_SUB
    assert n_pad % 8 == 0
    sched = _moe_schedule(logits[:, :n_exp], n_exp, n_pad)
    y4 = _moe(xnp, sched, w_up[0], b_up[0], w_down[0], b_down[0], n_rows_out=TOP_K * n_pad)
    y4 = y4.reshape(TOP_K, n_pad, d)
    g_fin = norm_final_g.reshape(1, d)
    y_p = _final(x2, y4, g_fin, tm=256, row0=0, n=npr).reshape(bp, sp, d)
    y_s = _final(x2, y4, g_fin, tm=nsm, row0=npr, n=nsm).reshape(bs, ss, d)

    kp = kvf_p.reshape(bp, sp, 2, N_GROUPS, hpg, HEAD_DIM)
    ksm = kvf_s.reshape(bs, ss, 2, N_GROUPS, hpg, HEAD_DIM)
    kv_prompt = [kp[:, sp - min(w, sp):, :, g][None] for g, w in enumerate(ATTN_WINDOWS)]
    kv_sample = [ksm[:, :, :, g][None] for g in range(N_GROUPS)]
    v_sgu = va_s.reshape(1, bs, ss, sgu_w)
    return (y_p, y_s, *kv_prompt, *kv_sample, v_sgu)
```

```python
import functools

import jax
import jax.numpy as jnp
import numpy as np
from jax import lax
from jax.experimental import pallas as pl
from jax.experimental.pallas import tpu as pltpu

F32 = jnp.float32
BF16 = jnp.bfloat16
U32 = jnp.uint32

HEAD_DIM = 64
ROT_DIM = HEAD_DIM // 4
ROPE_THETA = 500000.0
ATTN_WINDOWS = (128, 512, 2048)
ATTN_DILATIONS = (1, 4, 16)
N_GROUPS = 3
CHUNK = 128
ATTN_BLOCK = 128
TOP_K = 4
SWIGLU_LIMIT = 7.0
SWIGLU_ALPHA = 1.702
EPS = 1e-5
PAST_LEN = 8192

LANES = 128
VMEM_LIMIT = 56 * 1024 * 1024

MOE_SUB = 256
MOE_NSUB = 4
MOE_TM = MOE_NSUB * MOE_SUB
MOE_RING = 2 * MOE_NSUB
MOE_TF = 512

NT_DIMS = (((1,), (1,)), ((), ()))


def _cparams(sem):
    return pltpu.CompilerParams(dimension_semantics=sem, vmem_limit_bytes=VMEM_LIMIT)


def _gelu(x):
    return 0.5 * x * (1.0 + lax.erf(x * np.float32(np.sqrt(0.5))))


def _split2(a):
    a1 = a.astype(BF16)
    return a1, (a - a1.astype(F32)).astype(BF16)


def _pack_bf16_pairs(x):
    bits = lax.bitcast_convert_type(x.astype(BF16).astype(F32), U32)
    w = bits.shape[1] // 2
    return bits[:, :w] | (bits[:, w:] >> 16)


def _store_slabs(ref, x):
    for c in range(ref.shape[0]):
        ref[c] = x[:, c * LANES:(c + 1) * LANES]


def _load_slabs(ref, rows=slice(None)):
    return jnp.concatenate([ref[c, rows, :] for c in range(ref.shape[0])], axis=1)


def _unpack_bf16_pairs(w):
    hi = lax.bitcast_convert_type(w & jnp.uint32(0xFFFF0000), F32).astype(BF16)
    lo = lax.bitcast_convert_type(w << 16, F32).astype(BF16)
    return hi, lo


def _in_proj_kernel(x_ref, g_ref, w_ref, cos_ref, sa_ref, sb_ref, lng_ref, lnb_ref,
                    qkv_ref, u_ref, va_ref, *rest, nq, tpb, lws):
    kv_refs, xn_sc = rest[:-1], rest[-1]
    i = pl.program_id(0)
    j = pl.program_id(1)
    tm = x_ref.shape[0]

    @pl.when(j == 0)
    def _():
        x = x_ref[...]
        ms = jnp.mean(x * x, axis=-1, keepdims=True)
        xn_sc[...] = (x * lax.rsqrt(ms + EPS) * g_ref[...]).astype(BF16)

    acc = jnp.dot(xn_sc[...], w_ref[...], preferred_element_type=F32)

    def emit_kv(r):
        if lws is None:
            kv_refs[0][...] = r
            return
        ii = i % tpb
        for g, lw in enumerate(lws):
            @pl.when(((j == nq + g) | (j == 2 * nq + g)) & (ii >= tpb - max(lw // tm, 1)))
            def _():
                kv_refs[g][...] = r[tm - min(lw, tm):, :].T

    @pl.when(j < 2 * nq)
    def _():
        cos, sa, sb = cos_ref[...], sa_ref[...], sb_ref[...]
        parts = []
        for s in range(acc.shape[1] // LANES):
            z = acc[:, s * LANES:(s + 1) * LANES]
            parts.append(z * cos + pltpu.roll(z, ROT_DIM // 2, 1) * sa
                         + pltpu.roll(z, LANES - ROT_DIM // 2, 1) * sb)
        r = jnp.concatenate(parts, axis=1)
        _store_slabs(qkv_ref, _pack_bf16_pairs(r))

        @pl.when(j >= nq)
        def _():
            emit_kv(r)

    @pl.when((j >= 2 * nq) & (j < 3 * nq))
    def _():
        _store_slabs(qkv_ref, _pack_bf16_pairs(acc))
        emit_kv(acc)

    @pl.when(j == 3 * nq)
    def _():
        u_ref[...] = _gelu(acc).astype(BF16)

    @pl.when(j == 3 * nq + 1)
    def _():
        t = _gelu(acc)
        mu = jnp.mean(t, axis=-1, keepdims=True)
        var = jnp.mean(jnp.square(t - mu), axis=-1, keepdims=True)
        va_ref[...] = (t - mu) * lax.rsqrt(var + EPS) * lng_ref[...] + lnb_ref[...]


def _in_proj(x, g, w_bf16, cos_t, sa_t, sb_t, ln_g, ln_b, *, tm, attn_w, sgu_w, seqs=None):
    n, d = x.shape
    bw = sgu_w
    nq = attn_w // bw
    nj = 3 * nq + 2
    assert w_bf16.shape[1] == nj * bw and bw % (2 * LANES) == 0
    row = lambda i, j: (i, 0)
    out_specs = [
        pl.BlockSpec((bw // 2 // LANES, tm, LANES), lambda i, j: (jnp.minimum(j, 3 * nq - 1), i, 0)),
        pl.BlockSpec((tm, bw), row),
        pl.BlockSpec((tm, bw), row),
    ]
    out_shape = [
        jax.ShapeDtypeStruct((3 * attn_w // 2 // LANES, n, LANES), U32),
        jax.ShapeDtypeStruct((n, sgu_w), BF16),
        jax.ShapeDtypeStruct((n, sgu_w), F32),
    ]
    if seqs is None:
        tpb, lws = 1, None
        out_specs.append(pl.BlockSpec((tm, bw), lambda i, j: (i, jnp.clip(j - nq, 0, 2 * nq - 1))))
        out_shape.append(jax.ShapeDtypeStruct((n, 2 * attn_w), F32))
    else:
        n_seq, sl = seqs
        assert n == n_seq * sl and sl % tm == 0 and nq == N_GROUPS
        tpb = sl // tm
        lws = tuple(min(w, sl) for w in ATTN_WINDOWS)
        for gi, lw in enumerate(lws):
            assert lw % tm == 0 or tm % lw == 0
            nt = max(lw // tm, 1)

            def kv_map(i, j, gi=gi, nt=nt):
                ii = i % tpb
                inside = ii >= tpb - nt
                spare = n_seq + (i // tpb) * (tpb - nt) + ii
                return (jnp.where(inside, i // tpb, spare), jnp.where(j < 2 * nq + gi, 0, 1),
                        jnp.where(inside, ii - (tpb - nt), 0))
            out_specs.append(pl.BlockSpec((None, bw, min(lw, tm)), kv_map))
            out_shape.append(jax.ShapeDtypeStruct(((1 + tpb - nt) * n_seq, 2 * bw, lw), F32))
    return pl.pallas_call(
        functools.partial(_in_proj_kernel, nq=nq, tpb=tpb, lws=lws),
        grid=(pl.cdiv(n, tm), nj),
        in_specs=[
            pl.BlockSpec((tm, d), row),
            pl.BlockSpec((1, d), lambda i, j: (0, 0)),
            pl.BlockSpec((d, bw), lambda i, j: (0, j)),
            pl.BlockSpec((tm, LANES), row),
            pl.BlockSpec((tm, LANES), row),
            pl.BlockSpec((tm, LANES), row),
            pl.BlockSpec((1, bw), lambda i, j: (0, 0)),
            pl.BlockSpec((1, bw), lambda i, j: (0, 0)),
        ],
        out_specs=out_specs,
        out_shape=out_shape,
        scratch_shapes=[pltpu.VMEM((tm, d), BF16)],
        compiler_params=_cparams(("arbitrary", "arbitrary")),
        name="in_proj",
    )(x, g, w_bf16, cos_t, sa_t, sb_t, ln_g, ln_b)


def _rope_tables(pos):
    half = ROT_DIM // 2
    inv = ROPE_THETA ** (-jnp.arange(0, ROT_DIM, 2, dtype=F32) / ROT_DIM)
    ang = pos.astype(F32)[:, None] * inv[None, :]
    cos, sin = jnp.cos(ang), jnp.sin(ang)
    rows = pos.shape[0]
    one = jnp.ones((rows, HEAD_DIM - ROT_DIM), F32)
    zero = jnp.zeros((rows, HEAD_DIM - ROT_DIM), F32)
    zh = jnp.zeros((rows, half), F32)
    cos_h = jnp.concatenate([cos, cos, one], axis=1)
    sa_h = jnp.concatenate([zh, sin, zero], axis=1)
    sb_h = jnp.concatenate([-sin, zh, zero], axis=1)
    rep = LANES // HEAD_DIM
    return tuple(jnp.tile(t, (1, rep)) for t in (cos_h, sa_h, sb_h))


def _attn_prompt_kernel(*refs, nk, dil, has_prev):
    if has_prev:
        q_ref, kp_ref, kc_ref, vp_ref, vc_ref, o_ref, lse_ref = refs
    else:
        q_ref, kc_ref, vc_ref, o_ref, lse_ref = refs
    n = pl.program_id(1)
    tq = ATTN_BLOCK
    tk = 2 * tq if has_prev else tq
    qi = lax.broadcasted_iota(jnp.int32, (tq, tk), 0)
    kj = lax.broadcasted_iota(jnp.int32, (tq, tk), 1)
    if has_prev:
        dist = qi + tq - kj
        valid = (dist >= 0) & (dist <= nk) & ((n - 1) * tq + kj >= 0)
    else:
        dist = qi - kj
        valid = (dist >= 0) & (dist <= nk)
    hph = q_ref.shape[0] * LANES // HEAD_DIM

    def residue(r, carry):
        rows = pl.ds(r, tq, stride=dil) if dil > 1 else pl.ds(0, tq)
        q2 = _unpack_bf16_pairs(_load_slabs(q_ref, rows))
        k2 = _unpack_bf16_pairs(_load_slabs(kc_ref, rows))
        v2 = _unpack_bf16_pairs(_load_slabs(vc_ref, rows))
        if has_prev:
            kp2 = _unpack_bf16_pairs(_load_slabs(kp_ref, rows))
            vp2 = _unpack_bf16_pairs(_load_slabs(vp_ref, rows))
            k2 = [jnp.concatenate([a, b], axis=0) for a, b in zip(kp2, k2)]
            v2 = [jnp.concatenate([a, b], axis=0) for a, b in zip(vp2, v2)]
        outs, lses = [], []
        for h in range(2 * hph):
            t, c0 = divmod(h, hph)
            sl = slice(c0 * HEAD_DIM, (c0 + 1) * HEAD_DIM)
            s = lax.dot_general(q2[t][:, sl], k2[t][:, sl], NT_DIMS,
                                preferred_element_type=F32) * (HEAD_DIM ** -0.5)
            s = jnp.where(valid, s, -jnp.inf)
            m = jnp.max(s, axis=-1, keepdims=True)
            p = jnp.exp(s - m)
            den = jnp.sum(p, axis=-1, keepdims=True)
            outs.append(jnp.dot(p.astype(BF16), v2[t][:, sl], preferred_element_type=F32) / den)
            lses.append(jnp.broadcast_to(m + jnp.log(den), (tq, HEAD_DIM)))
        o = jnp.concatenate(outs, axis=1)
        lse = jnp.concatenate(lses, axis=1)
        for c in range(o_ref.shape[0]):
            o_ref[c, rows, :] = o[:, c * LANES:(c + 1) * LANES]
            lse_ref[c, rows, :] = lse[:, c * LANES:(c + 1) * LANES]
        return carry

    if dil > 1:
        lax.fori_loop(0, dil, residue, 0)
    else:
        residue(0, 0)


def _attn_prompt(qkvp, g, *, bp, sp, gw):
    dil, win = ATTN_DILATIONS[g], ATTN_WINDOWS[g]
    nk = win // dil
    tile = ATTN_BLOCK * dil
    assert nk <= ATTN_BLOCK and sp % tile == 0
    nb = sp // tile
    has_prev = nb > 1
    blk = (gw // 2 // LANES, tile, LANES)
    cur = lambda col: pl.BlockSpec(blk, lambda b, n: (col, b * nb + n, 0))
    prev = lambda col: pl.BlockSpec(blk, lambda b, n: (col, b * nb + jnp.maximum(n - 1, 0), 0))
    kc, vc = N_GROUPS + g, 2 * N_GROUPS + g
    in_specs = [cur(g), prev(kc), cur(kc), prev(vc), cur(vc)] if has_prev else [cur(g), cur(kc), cur(vc)]
    out_spec = pl.BlockSpec((gw // LANES, tile, LANES), lambda b, n: (0, b * nb + n, 0))
    return pl.pallas_call(
        functools.partial(_attn_prompt_kernel, nk=nk, dil=dil, has_prev=has_prev),
        grid=(bp, nb),
        in_specs=in_specs,
        out_specs=[out_spec, out_spec],
        out_shape=[jax.ShapeDtypeStruct((gw // LANES, bp * sp, LANES), F32)] * 2,
        compiler_params=_cparams(("arbitrary", "arbitrary")),
        name=f"attn_prompt_g{g}",
    )(*([qkvp] * len(in_specs)))


def _attn_sample_kernel(q_ref, kvn_ref, s0_ref, s1_ref, s2_ref, o_ref, lse_ref, *, attn_w, gw):
    rows = 16
    ri = lax.broadcasted_iota(jnp.int32, (rows, gw), 0)
    lh = lax.broadcasted_iota(jnp.int32, (rows, gw), 1) // HEAD_DIM
    diag = ri == lh
    scale = HEAD_DIM ** -0.5
    kvn = kvn_ref[...]
    hw = gw // 2
    d = functools.partial(jnp.dot, preferred_element_type=F32)
    dn = functools.partial(lax.dot_general, dimension_numbers=NT_DIMS, preferred_element_type=F32)
    o_parts, lse_parts = [], []
    for g, s_ref in enumerate((s0_ref, s1_ref, s2_ref)):
        dil = ATTN_DILATIONS[g]
        nk = ATTN_WINDOWS[g] // dil
        lb = s_ref.shape[1]
        q = jnp.concatenate(_unpack_bf16_pairs(q_ref[:, g * hw:(g + 1) * hw]), axis=1).astype(F32)
        k_new = kvn[:, g * gw:(g + 1) * gw]
        v_new = kvn[:, attn_w + g * gw:attn_w + (g + 1) * gw]
        qblk = jnp.where(diag, jnp.broadcast_to(q, (rows, gw)), 0.0)
        q1, q2 = _split2(qblk)
        k1, k2 = _split2(s_ref[:gw, :])
        s = (d(q1, k1) + d(q2, k1) + d(q1, k2)) * scale
        back = lb - lax.broadcasted_iota(jnp.int32, (rows, lb), 1)
        valid = ((back & (dil - 1)) == 0) & (back <= dil * nk)
        s = jnp.where(valid, s, -jnp.inf)
        s_new = jnp.sum(qblk * k_new, axis=1, keepdims=True) * scale
        m = jnp.maximum(jnp.max(s, axis=1, keepdims=True), s_new)
        p = jnp.exp(s - m)
        p_new = jnp.exp(s_new - m)
        den = jnp.sum(p, axis=1, keepdims=True) + p_new
        lse = m + jnp.log(den)
        p1, p2 = _split2(p)
        v1, v2 = _split2(s_ref[gw:, :])
        o_full = (dn(p1, v1) + dn(p2, v1) + dn(p1, v2) + p_new * v_new) / den
        o_parts.append(jnp.sum(jnp.where(diag, o_full, 0.0), axis=0, keepdims=True))
        lse_parts.append(jnp.sum(jnp.where(diag, jnp.broadcast_to(lse, (rows, gw)), 0.0), axis=0, keepdims=True))
    o_ref[...] = jnp.concatenate(o_parts, axis=1)
    lse_ref[...] = jnp.concatenate(lse_parts, axis=1)


def _attn_sample(qkvp_s, kvf_s, states, *, attn_w, gw):
    bs = qkvp_s.shape[0]
    views, specs = [], []
    for g, st in enumerate(states):
        dil, win = ATTN_DILATIONS[g], ATTN_WINDOWS[g]
        lb = st.shape[1]
        assert lb == win and (dil & (dil - 1)) == 0 and st.shape[0] == bs
        views.append(jnp.transpose(st, (0, 2, 3, 4, 1)).reshape(bs, 2 * gw, lb))
        specs.append(pl.BlockSpec((None, 2 * gw, lb), lambda b: (b, 0, 0)))
    row3 = lambda w: pl.BlockSpec((None, 1, w), lambda b: (b, 0, 0))
    o, lse = pl.pallas_call(
        functools.partial(_attn_sample_kernel, attn_w=attn_w, gw=gw),
        grid=(bs,),
        in_specs=[row3(qkvp_s.shape[1]), row3(2 * attn_w), *specs],
        out_specs=[row3(attn_w), row3(attn_w)],
        out_shape=[jax.ShapeDtypeStruct((bs, 1, attn_w), F32),
                   jax.ShapeDtypeStruct((bs, 1, attn_w), F32)],
        compiler_params=_cparams(("arbitrary",)),
        name="attn_sample",
    )(qkvp_s.reshape(bs, 1, -1), kvf_s.reshape(bs, 1, 2 * attn_w), *views)
    return o.reshape(bs, attn_w), lse.reshape(bs, attn_w)


def _sgu_kernel(u_ref, va_ref, w_ref, bias_ref, g_ref, o_ref, *, cps):
    ng = w_ref.shape[0]
    ti = lax.broadcasted_iota(jnp.int32, (CHUNK, CHUNK), 0)
    si = lax.broadcasted_iota(jnp.int32, (CHUNK, CHUNK), 1)
    wt = [jnp.where(ti >= si, w_ref[g], 0.0).astype(BF16) for g in range(ng)]
    cw = va_ref.shape[1] // ng
    for c in range(cps):
        rows = slice(c * CHUNK, (c + 1) * CHUNK)
        va = va_ref[rows, :].astype(BF16)
        f = jnp.concatenate(
            [jnp.dot(wt[g], va[:, g * cw:(g + 1) * cw], preferred_element_type=F32) for g in range(ng)],
            axis=1) + bias_ref[...]
        s = u_ref[rows, :].astype(F32) * f
        ms = jnp.mean(s * s, axis=-1, keepdims=True)
        o_ref[rows, :] = (s * lax.rsqrt(ms + EPS) * g_ref[...]).astype(BF16)


def _sgu_prompt(u, va, w_sp, bias_tc, g_a, *, cps):
    n, w = u.shape
    tm = cps * CHUNK
    assert n % tm == 0
    row = lambda i: (i, 0)
    return pl.pallas_call(
        functools.partial(_sgu_kernel, cps=cps),
        grid=(n // tm,),
        in_specs=[pl.BlockSpec((tm, w), row), pl.BlockSpec((tm, w), row),
                  pl.BlockSpec(w_sp.shape, lambda i: (0, 0, 0)),
                  pl.BlockSpec((CHUNK, w), lambda i: (0, 0)),
                  pl.BlockSpec((1, w), lambda i: (0, 0))],
        out_specs=pl.BlockSpec((tm, w), row),
        out_shape=jax.ShapeDtypeStruct((n, w), BF16),
        compiler_params=_cparams(("arbitrary",)),
        name="sgu_prompt",
    )(u, va, w_sp, bias_tc, g_a)


def _sgu_sample_kernel(u_ref, va_ref, w0_ref, b0_ref, g_ref, o_ref):
    s = u_ref[...].astype(F32) * (va_ref[...] * w0_ref[...] + b0_ref[...])
    ms = jnp.mean(s * s, axis=-1, keepdims=True)
    o_ref[...] = (s * lax.rsqrt(ms + EPS) * g_ref[...]).astype(BF16)


def _sgu_sample(u, va, w0, b0, g_a):
    n, w = u.shape
    full = pl.BlockSpec((n, w), lambda i: (0, 0))
    vec = pl.BlockSpec((1, w), lambda i: (0, 0))
    return pl.pallas_call(
        _sgu_sample_kernel, grid=(1,),
        in_specs=[full, full, vec, vec, vec], out_specs=full,
        out_shape=jax.ShapeDtypeStruct((n, w), BF16),
        name="sgu_sample",
    )(u, va, w0, b0, g_a)


def _out_proj_kernel(*refs, aliased):
    ng = N_GROUPS
    o_refs, lse_refs = refs[:ng], refs[ng:2 * ng]
    rest = refs[2 * ng:]
    if aliased:
        rest = rest[:7] + rest[10:]
    sgu_ref, x_ref, gb_ref, wout_ref, gffn_ref, wr_ref, br_ref, x2_ref, xnp_ref, lg_ref = rest
    ls = [_load_slabs(r) for r in lse_refs]
    m = functools.reduce(jnp.maximum, ls)
    es = [jnp.exp(l - m) for l in ls]
    inv = 1.0 / functools.reduce(jnp.add, es)
    attn = [_load_slabs(o_refs[g]) * (es[g] * inv) for g in range(ng)]
    gw = attn[0].shape[1]
    ms = functools.reduce(jnp.add, [jnp.sum(a * a, axis=-1, keepdims=True) for a in attn]) / (ng * gw)
    rstd = lax.rsqrt(ms + EPS)
    y = jnp.dot(sgu_ref[...], wout_ref[ng * gw:, :], preferred_element_type=F32)
    for g in range(ng):
        a_n = (attn[g] * rstd * gb_ref[:, g * gw:(g + 1) * gw]).astype(BF16)
        y = y + jnp.dot(a_n, wout_ref[g * gw:(g + 1) * gw, :], preferred_element_type=F32)
    x2 = x_ref[...] + y
    x2_ref[...] = x2
    ms2 = jnp.mean(x2 * x2, axis=-1, keepdims=True)
    xn = x2 * lax.rsqrt(ms2 + EPS) * gffn_ref[...]
    x1, xr = _split2(xn)
    d = functools.partial(jnp.dot, preferred_element_type=F32)
    lg_ref[...] = d(x1, wr_ref[0]) + d(x1, wr_ref[1]) + d(xr, wr_ref[0]) + br_ref[...]
    xnp_ref[...] = _pack_bf16_pairs(xn)


def _out_proj(o_l, lse_l, sgu_n, x, g_b, wout_bf16, g_ffn, wr_hl, b_r, *, tm, n_total, row0, prev=None):
    n, d = x.shape
    gw = o_l[0].shape[0] * LANES
    assert n % tm == 0 and row0 % tm == 0 and len(o_l) == N_GROUPS
    off = row0 // tm
    row = lambda i: (i, 0)
    orow = lambda i: (i + off, 0)
    const2 = lambda i: (0, 0)
    in_specs = [pl.BlockSpec((gw // LANES, tm, LANES), lambda i: (0, i, 0))] * (2 * N_GROUPS) + [
        pl.BlockSpec((tm, sgu_n.shape[1]), row),
        pl.BlockSpec((tm, d), row), pl.BlockSpec((1, N_GROUPS * gw), const2),
        pl.BlockSpec((d, d), const2), pl.BlockSpec((1, d), const2),
        pl.BlockSpec((2, d, LANES), lambda i: (0, 0, 0)), pl.BlockSpec((1, LANES), const2),
    ]
    args = [*o_l, *lse_l, sgu_n, x, g_b, wout_bf16, g_ffn, wr_hl, b_r]
    aliases = {}
    if prev is not None:
        in_specs += [pl.BlockSpec(memory_space=pl.ANY)] * 3
        aliases = {len(args) + k: k for k in range(3)}
        args += list(prev)
    return pl.pallas_call(
        functools.partial(_out_proj_kernel, aliased=prev is not None),
        grid=(n // tm,),
        in_specs=in_specs,
        out_specs=[pl.BlockSpec((tm, d), orow), pl.BlockSpec((tm, d // 2), orow),
                   pl.BlockSpec((tm, LANES), orow)],
        out_shape=[jax.ShapeDtypeStruct((n_total, d), F32),
                   jax.ShapeDtypeStruct((n_total, d // 2), U32),
                   jax.ShapeDtypeStruct((n_total, LANES), F32)],
        input_output_aliases=aliases,
        compiler_params=_cparams(("arbitrary",)),
        name="out_proj",
    )(*args)


def _moe_kernel(be_ref, ns_ref, sb_ref, pro_ref, pref_ref, dst_ref, gate_ref, xn_hbm,
                wg_ref, wl_ref, bg_ref, bl_ref, wd_ref, bd_ref, out_hbm,
                xb_sc, acc_sc, wg_sc, wl_sc, wd_sc, sem_in, sem_out, *, nc):
    b = pl.program_id(0)
    c = pl.program_id(1)
    nb = pl.num_programs(0)
    ns = ns_ref[b]
    s0 = sb_ref[b]
    del be_ref
    ring_rows = MOE_RING * MOE_SUB
    per_it = MOE_SUB // nc
    half = xb_sc.shape[1]
    d = functools.partial(jnp.dot, preferred_element_type=F32)

    def row_in(tok, ring_row, slot):
        return pltpu.make_async_copy(xn_hbm.at[pl.ds(tok, 1), :], xb_sc.at[pl.ds(ring_row, 1), :], sem_in.at[slot])

    def sub_in_wait(slot):
        pltpu.make_async_copy(xn_hbm.at[pl.ds(0, MOE_SUB), :], xb_sc.at[pl.ds(0, MOE_SUB), :],
                              sem_in.at[slot]).wait()

    def sub_out_wait():
        pltpu.make_async_copy(acc_sc.at[pl.ds(0, MOE_SUB), :], out_hbm.at[pl.ds(0, MOE_SUB), :], sem_out).wait()

    def rows_of(s):
        return pl.ds(pl.multiple_of(s * MOE_SUB, MOE_SUB), MOE_SUB)

    def compute(s):
        i0 = (c * ns + s) * per_it
        p0 = (s0 + MOE_NSUB) * MOE_SUB + i0
        slot = (p0 // MOE_SUB) % MOE_RING
        q0 = p0 % ring_rows
        for k in range(per_it):
            row_in(pref_ref[0, i0 + k], q0 + k, slot).start()
        x0 = pl.multiple_of(((s0 + s) % MOE_RING) * MOE_SUB, MOE_SUB)
        x_hi, x_lo = _unpack_bf16_pairs(xb_sc[pl.ds(x0, MOE_SUB), :])
        hg = d(x_hi, wg_sc[:half, :]) + d(x_lo, wg_sc[half:, :]) + bg_ref[...]
        hl = d(x_hi, wl_sc[:half, :]) + d(x_lo, wl_sc[half:, :]) + bl_ref[...]
        glu = jnp.minimum(hg, SWIGLU_LIMIT)
        lin = jnp.clip(hl, -SWIGLU_LIMIT, SWIGLU_LIMIT)
        a = glu * jax.nn.sigmoid(SWIGLU_ALPHA * glu) * (lin + 1.0)
        return d(a.astype(BF16), wd_sc[...])

    def scatter(s):
        a0 = s * MOE_SUB
        for k in range(MOE_SUB):
            pltpu.make_async_copy(acc_sc.at[pl.ds(a0 + k, 1), :],
                                  out_hbm.at[pl.ds(dst_ref[0, a0 + k], 1), :], sem_out).start()

    @pl.when((b == 0) & (c == 0))
    def _():
        def start(i, carry):
            row_in(pro_ref[0, i], i, i // MOE_SUB).start()
            return carry
        lax.fori_loop(0, MOE_TM, start, 0)

    @pl.when(ns > 0)
    def _():
        @pl.when(c == 0)
        def _():
            for s in range(MOE_NSUB):
                @pl.when(s < ns)
                def _():
                    sub_in_wait((s0 + s) % MOE_RING)

        wg_sc[...] = wg_ref[...].astype(BF16)
        wl_sc[...] = wl_ref[...].astype(BF16)
        wd_sc[...] = wd_ref[...].astype(BF16)

        @pl.when(c == 0)
        def _():
            def body(s, carry):
                acc_sc[rows_of(s), :] = compute(s) + bd_ref[...]
                return carry
            lax.fori_loop(0, ns, body, 0)

        @pl.when((c > 0) & (c < nc - 1))
        def _():
            def body(s, carry):
                acc_sc[rows_of(s), :] += compute(s)
                return carry
            lax.fori_loop(0, ns, body, 0)

        @pl.when(c == nc - 1)
        def _():
            def finish(s):
                r = rows_of(s)
                acc_sc[r, :] = (acc_sc[r, :] + compute(s)) * gate_ref[r, :]

            finish(0)

            def body(s, carry):
                finish(s)
                scatter(s - 1)
                return carry
            lax.fori_loop(1, ns, body, 0)
            scatter(ns - 1)

            def wait(s, carry):
                sub_out_wait()
                return carry
            lax.fori_loop(0, ns, wait, 0)

    @pl.when((b == nb - 1) & (c == nc - 1))
    def _():
        total = sb_ref[nb - 1] + ns_ref[nb - 1]
        for k in range(MOE_NSUB):
            sub_in_wait((total + k) % MOE_RING)


def _moe(xnp, sched, w_up, b_up, w_down, b_down, *, n_rows_out):
    be, nsub, sub0, pro_tbl, pref_tbl, dst_tbl, gate_tbl = sched
    n_tok, half = xnp.shape
    d = 2 * half
    n_exp, _, ff2 = w_up.shape
    ff = ff2 // 2
    assert ff % MOE_TF == 0 and MOE_TF % LANES == 0
    nc = ff // MOE_TF
    assert nc > 2 and MOE_SUB % nc == 0 and (MOE_RING * MOE_SUB) % (MOE_SUB // nc) == 0
    nbmax = be.shape[0]

    def cidx(b, c, ns):
        return jnp.where(ns[b] > 0, c, nc - 1)

    smem_tbl = lambda: pl.BlockSpec((None, 1, MOE_TM), lambda b, c, be, ns, sb: (b, 0, 0), memory_space=pltpu.SMEM)
    grid_spec = pltpu.PrefetchScalarGridSpec(
        num_scalar_prefetch=3,
        grid=(nbmax, nc),
        in_specs=[
            pl.BlockSpec((1, MOE_TM), lambda b, c, be, ns, sb: (0, 0), memory_space=pltpu.SMEM),
            smem_tbl(),
            smem_tbl(),
            pl.BlockSpec((None, MOE_TM, 1), lambda b, c, be, ns, sb: (b, 0, 0)),
            pl.BlockSpec(memory_space=pl.ANY),
            pl.BlockSpec((None, d, MOE_TF), lambda b, c, be, ns, sb: (be[b], 0, cidx(b, c, ns))),
            pl.BlockSpec((None, d, MOE_TF), lambda b, c, be, ns, sb: (be[b], 0, nc + cidx(b, c, ns))),
            pl.BlockSpec((None, 1, MOE_TF), lambda b, c, be, ns, sb: (be[b], 0, cidx(b, c, ns))),
            pl.BlockSpec((None, 1, MOE_TF), lambda b, c, be, ns, sb: (be[b], 0, nc + cidx(b, c, ns))),
            pl.BlockSpec((None, MOE_TF, d), lambda b, c, be, ns, sb: (be[b], cidx(b, c, ns), 0)),
            pl.BlockSpec((None, 1, d), lambda b, c, be, ns, sb: (be[b], 0, 0)),
        ],
        out_specs=pl.BlockSpec(memory_space=pl.ANY),
        scratch_shapes=[
            pltpu.VMEM((MOE_RING * MOE_SUB, half), U32),
            pltpu.VMEM((MOE_TM, d), F32),
            pltpu.VMEM((d, MOE_TF), BF16),
            pltpu.VMEM((d, MOE_TF), BF16),
            pltpu.VMEM((MOE_TF, d), BF16),
            pltpu.SemaphoreType.DMA((MOE_RING,)),
            pltpu.SemaphoreType.DMA,
        ],
    )
    return pl.pallas_call(
        functools.partial(_moe_kernel, nc=nc),
        grid_spec=grid_spec,
        out_shape=jax.ShapeDtypeStruct((n_rows_out, d), F32),
        compiler_params=_cparams(("arbitrary", "arbitrary")),
        name="moe",
    )(be, nsub, sub0, pro_tbl, pref_tbl, dst_tbl, gate_tbl, xnp, w_up, w_up,
      b_up.reshape(n_exp, 1, ff2), b_up.reshape(n_exp, 1, ff2), w_down, b_down.reshape(n_exp, 1, d))


def _moe_schedule(logits, n_exp, n_pad):
    n = logits.shape[0]
    i32 = jnp.int32
    top_v, top_i = lax.top_k(logits, TOP_K)
    gates = jax.nn.softmax(top_v, axis=-1)
    na = n * TOP_K
    flat_e = top_i.reshape(-1).astype(i32)
    flat_g = gates.reshape(-1)
    order = jnp.argsort(flat_e, stable=True).astype(i32)
    counts = jnp.bincount(flat_e, length=n_exp).astype(i32)
    starts = jnp.cumsum(counts) - counts
    ns_e = (counts + MOE_SUB - 1) // MOE_SUB
    nblk_e = (ns_e + MOE_NSUB - 1) // MOE_NSUB
    spb_e = -(-ns_e // jnp.maximum(nblk_e, 1))
    sub_ends = jnp.cumsum(ns_e)
    sub_starts = sub_ends - ns_e
    total_subs = sub_ends[-1]
    bends = jnp.cumsum(nblk_e)
    total_blocks = bends[-1]

    nsub_max = -(-na // MOE_SUB) + n_exp
    nbmax = -(-nsub_max // MOE_NSUB) + n_exp
    bid = jnp.arange(nbmax, dtype=i32)
    used = bid < total_blocks
    e_b = jnp.minimum(jnp.searchsorted(bends, jnp.minimum(bid, total_blocks - 1), side="right"),
                      n_exp - 1).astype(i32)
    jb = bid - (bends[e_b] - nblk_e[e_b])
    sub0 = jnp.where(used, sub_starts[e_b] + jb * spb_e[e_b], total_subs).astype(i32)
    nsub = jnp.where(used, jnp.clip(ns_e[e_b] - jb * spb_e[e_b], 0, spb_e[e_b]), 0).astype(i32)

    gs = nsub_max + 2 * MOE_NSUB
    gid = jnp.arange(gs, dtype=i32)
    e_g = jnp.minimum(jnp.searchsorted(sub_ends, gid, side="right"), n_exp - 1).astype(i32)
    lane = jnp.arange(MOE_SUB, dtype=i32)[None, :]
    within = (gid - sub_starts[e_g])[:, None] * MOE_SUB + lane
    valid = (gid < total_subs)[:, None] & (within < counts[e_g][:, None])
    a = order[jnp.clip(starts[e_g][:, None] + within, 0, na - 1)]
    tok_f = jnp.where(valid, a // TOP_K, 0).astype(i32).reshape(-1)
    dst_f = jnp.where(valid, (a % TOP_K) * n_pad + a // TOP_K, n + lane).astype(i32).reshape(-1)
    gate_f = jnp.where(valid, flat_g[a], 0.0).astype(F32).reshape(-1)

    own = sub0[:, None] * MOE_SUB + jnp.arange(MOE_TM, dtype=i32)[None, :]
    dst_tbl = dst_f[own].reshape(nbmax, 1, MOE_TM)
    gate_tbl = gate_f[own].reshape(nbmax, MOE_TM, 1)
    pref_tbl = tok_f[own + MOE_TM].reshape(nbmax, 1, MOE_TM)
    pro_tbl = tok_f[:MOE_TM].reshape(1, MOE_TM)
    return e_b, nsub, sub0, pro_tbl, pref_tbl, dst_tbl, gate_tbl


def _final_kernel(x2_ref, y4_ref, g_ref, o_ref):
    y = y4_ref[0]
    for k in range(1, TOP_K):
        y = y + y4_ref[k]
    x = x2_ref[...] + y
    ms = jnp.mean(x * x, axis=-1, keepdims=True)
    o_ref[...] = x * lax.rsqrt(ms + EPS) * g_ref[...]


def _final(x2, y4, g, *, tm, row0, n):
    d = x2.shape[1]
    assert n % tm == 0 and row0 % tm == 0
    off = row0 // tm
    return pl.pallas_call(
        _final_kernel, grid=(n // tm,),
        in_specs=[pl.BlockSpec((tm, d), lambda i: (i + off, 0)),
                  pl.BlockSpec((TOP_K, tm, d), lambda i: (0, i + off, 0)),
                  pl.BlockSpec((1, d), lambda i: (0, 0))],
        out_specs=pl.BlockSpec((tm, d), lambda i: (i, 0)),
        out_shape=jax.ShapeDtypeStruct((n, d), F32),
        compiler_params=_cparams(("arbitrary",)),
        name="final",
    )(x2, y4, g)


def kernel(x_prompt, x_sample, state_b0_kv, state_b1_kv, state_b2_kv, norm_mix_g, w_in, ln_va_g, ln_va_b,
           w_spatial, b_spatial, norm_out_b_g, norm_out_a_g, w_out, norm_ffn_g, w_router, b_router,
           w_up, b_up, w_down, b_down, norm_final_g):
    bp, sp, d = x_prompt.shape
    bs, ss, _ = x_sample.shape
    depth = w_in.shape[0]
    assert depth == 1 and ss == 1
    hpg = state_b0_kv.shape[4]
    gw = hpg * HEAD_DIM
    attn_w = N_GROUPS * gw
    sgu_w = ln_va_g.shape[1]
    n_exp = w_router.shape[2]
    npr, nsm = bp * sp, bs * ss
    n_total = npr + nsm
    states = (state_b0_kv[0], state_b1_kv[0], state_b2_kv[0])

    w_in_b = w_in[0].astype(BF16)
    w_out_b = w_out[0].astype(BF16)
    g_mix = norm_mix_g[0].reshape(1, d)
    ln_g, ln_b = ln_va_g[0].reshape(1, sgu_w), ln_va_b[0].reshape(1, sgu_w)
    g_b = norm_out_b_g[0].reshape(1, attn_w)
    g_a = norm_out_a_g[0].reshape(1, sgu_w)
    g_ffn = norm_ffn_g[0].reshape(1, d)
    ng = w_spatial.shape[1]
    cw = sgu_w // ng
    bias_tc = jnp.repeat(b_spatial[0].T, cw, axis=1)
    w00 = jnp.repeat(w_spatial[0][:, 0, 0], cw).reshape(1, sgu_w)
    b00 = jnp.repeat(b_spatial[0][:, 0], cw).reshape(1, sgu_w)
    wr = jnp.pad(w_router[0], ((0, 0), (0, LANES - n_exp)))
    wr_hl = jnp.stack(_split2(wr), 0)
    b_r = jnp.pad(b_router[0], (0, LANES - n_exp)).reshape(1, LANES)

    tabs_p = _rope_tables(jnp.tile(jnp.arange(sp), bp))
    tabs_s = _rope_tables(jnp.tile(PAST_LEN + jnp.arange(ss), bs))

    xp = x_prompt.reshape(npr, d)
    xs = x_sample.reshape(nsm, d)
    qkvp_p, u_p, va_p, *kvt_p = _in_proj(xp, g_mix, w_in_b, *tabs_p, ln_g, ln_b, tm=1024,
                                         attn_w=attn_w, sgu_w=sgu_w, seqs=(bp, sp))
    qkvp_s, u_s, va_s, kvf_s = _in_proj(xs, g_mix, w_in_b, *tabs_s, ln_g, ln_b, tm=nsm,
                                        attn_w=attn_w, sgu_w=sgu_w)

    o_p, lse_p = zip(*[_attn_prompt(qkvp_p, g, bp=bp, sp=sp, gw=gw) for g in range(N_GROUPS)])
    rows_s = jnp.transpose(qkvp_s, (1, 0, 2)).reshape(nsm, -1)
    o_s, lse_s = _attn_sample(rows_s, kvf_s, states, attn_w=attn_w, gw=gw)
    split = lambda t: [jnp.transpose(t[:, g * gw:(g + 1) * gw].reshape(nsm, gw // LANES, LANES), (1, 0, 2))
                       for g in range(N_GROUPS)]

    sgu_p = _sgu_prompt(u_p, va_p, w_spatial[0], bias_tc, g_a, cps=4)
    sgu_s = _sgu_sample(u_s, va_s, w00, b00, g_a)

    prev = _out_proj(o_p, lse_p, sgu_p, xp, g_b, w_out_b, g_ffn, wr_hl, b_r, tm=256, n_total=n_total, row0=0)
    x2, xnp, logits = _out_proj(split(o_s), split(lse_s), sgu_s, xs, g_b, w_out_b, g_ffn, wr_hl, b_r,
                                tm=nsm, n_total=n_total, row0=npr, prev=prev)

    n_pad = n_total + MOE_SUB
    assert n_pad % 8 == 0
    sched = _moe_schedule(logits[:, :n_exp], n_exp, n_pad)
    y4 = _moe(xnp, sched, w_up[0], b_up[0], w_down[0], b_down[0], n_rows_out=TOP_K * n_pad)
    y4 = y4.reshape(TOP_K, n_pad, d)
    g_fin = norm_final_g.reshape(1, d)
    y_p = _final(x2, y4, g_fin, tm=256, row0=0, n=npr).reshape(bp, sp, d)
    y_s = _final(x2, y4, g_fin, tm=nsm, row0=npr, n=nsm).reshape(bs, ss, d)

    kv_prompt = [jnp.transpose(t[:bp].reshape(bp, 2, hpg, HEAD_DIM, t.shape[2]), (0, 4, 1, 2, 3))[None]
                 for t in kvt_p]
    ksm = kvf_s.reshape(bs, ss, 2, N_GROUPS, hpg, HEAD_DIM)
    kv_sample = [ksm[:, :, :, g][None] for g in range(N_GROUPS)]
    v_sgu = va_s.reshape(1, bs, ss, sgu_w)
    return (y_p, y_s, *kv_prompt, *kv_sample, v_sgu)
```

```python
import functools

import jax
import jax.numpy as jnp
import numpy as np
from jax import lax
from jax.experimental import pallas as pl
from jax.experimental.pallas import tpu as pltpu

F32 = jnp.float32
BF16 = jnp.bfloat16
U32 = jnp.uint32

HEAD_DIM = 64
ROT_DIM = HEAD_DIM // 4
ROPE_THETA = 500000.0
ATTN_WINDOWS = (128, 512, 2048)
ATTN_DILATIONS = (1, 4, 16)
N_GROUPS = 3
CHUNK = 128
ATTN_BLOCK = 128
TOP_K = 4
SWIGLU_LIMIT = 7.0
SWIGLU_ALPHA = 1.702
EPS = 1e-5
PAST_LEN = 8192

LANES = 128
VMEM_LIMIT = 60 * 1024 * 1024

MOE_SUB = 256
MOE_NSUB = 5
MOE_TM = MOE_NSUB * MOE_SUB
MOE_RING = 2 * MOE_NSUB
MOE_TF = 512

NT_DIMS = (((1,), (1,)), ((), ()))


def _cparams(sem):
    return pltpu.CompilerParams(dimension_semantics=sem, vmem_limit_bytes=VMEM_LIMIT)


def _gelu(x):
    return 0.5 * x * (1.0 + lax.erf(x * np.float32(np.sqrt(0.5))))


def _split2(a):
    a1 = a.astype(BF16)
    return a1, (a - a1.astype(F32)).astype(BF16)


def _pack_bf16_pairs(x):
    bits = lax.bitcast_convert_type(x.astype(BF16).astype(F32), U32)
    w = bits.shape[1] // 2
    return bits[:, :w] | (bits[:, w:] >> 16)


def _store_slabs(ref, x):
    for c in range(ref.shape[0]):
        ref[c] = x[:, c * LANES:(c + 1) * LANES]


def _load_slabs(ref, rows=slice(None)):
    return jnp.concatenate([ref[c, rows, :] for c in range(ref.shape[0])], axis=1)


def _unpack_bf16_pairs(w):
    hi = lax.bitcast_convert_type(w & jnp.uint32(0xFFFF0000), F32).astype(BF16)
    lo = lax.bitcast_convert_type(w << 16, F32).astype(BF16)
    return hi, lo


def _in_proj_kernel(x_ref, g_ref, w_ref, cos_ref, sa_ref, sb_ref, lng_ref, lnb_ref,
                    qkv_ref, u_ref, va_ref, *rest, nq, tpb, lws):
    kv_refs, xn_sc = rest[:-1], rest[-1]
    i = pl.program_id(0)
    j = pl.program_id(1)
    tm = x_ref.shape[0]

    @pl.when(j == 0)
    def _():
        x = x_ref[...]
        ms = jnp.mean(x * x, axis=-1, keepdims=True)
        xn_sc[...] = (x * lax.rsqrt(ms + EPS) * g_ref[...]).astype(BF16)

    def matmul():
        return jnp.dot(xn_sc[...], w_ref[...], preferred_element_type=F32)

    def emit_kv(r):
        if lws is None:
            kv_refs[0][...] = r
            return
        ii = i % tpb
        for g, lw in enumerate(lws):
            @pl.when(((j == nq + g) | (j == 2 * nq + g)) & (ii >= tpb - max(lw // tm, 1)))
            def _():
                kv_refs[g][...] = r[tm - min(lw, tm):, :].T

    @pl.when(j < 2 * nq)
    def _():
        acc = matmul()
        cos, sa, sb = cos_ref[...], sa_ref[...], sb_ref[...]
        parts = []
        for s in range(acc.shape[1] // LANES):
            z = acc[:, s * LANES:(s + 1) * LANES]
            parts.append(z * cos + pltpu.roll(z, ROT_DIM // 2, 1) * sa
                         + pltpu.roll(z, LANES - ROT_DIM // 2, 1) * sb)
        r = jnp.concatenate(parts, axis=1)
        _store_slabs(qkv_ref, _pack_bf16_pairs(r))

        @pl.when(j >= nq)
        def _():
            emit_kv(r)

    @pl.when((j >= 2 * nq) & (j < 3 * nq))
    def _():
        acc = matmul()
        _store_slabs(qkv_ref, _pack_bf16_pairs(acc))
        emit_kv(acc)

    @pl.when(j == 3 * nq)
    def _():
        u_ref[...] = _gelu(matmul()).astype(BF16)

    @pl.when(j == 3 * nq + 1)
    def _():
        t = _gelu(matmul())
        mu = jnp.mean(t, axis=-1, keepdims=True)
        var = jnp.mean(jnp.square(t - mu), axis=-1, keepdims=True)
        va_ref[...] = (t - mu) * lax.rsqrt(var + EPS) * lng_ref[...] + lnb_ref[...]


def _in_proj(x, g, w_bf16, cos_t, sa_t, sb_t, ln_g, ln_b, *, tm, attn_w, sgu_w, seqs=None):
    n, d = x.shape
    bw = sgu_w
    nq = attn_w // bw
    nj = 3 * nq + 2
    assert w_bf16.shape[1] == nj * bw and bw % (2 * LANES) == 0
    row = lambda i, j: (i, 0)
    out_specs = [
        pl.BlockSpec((bw // 2 // LANES, tm, LANES), lambda i, j: (jnp.minimum(j, 3 * nq - 1), i, 0)),
        pl.BlockSpec((tm, bw), row),
        pl.BlockSpec((tm, bw), row),
    ]
    out_shape = [
        jax.ShapeDtypeStruct((3 * attn_w // 2 // LANES, n, LANES), U32),
        jax.ShapeDtypeStruct((n, sgu_w), BF16),
        jax.ShapeDtypeStruct((n, sgu_w), F32),
    ]
    if seqs is None:
        tpb, lws = 1, None
        out_specs.append(pl.BlockSpec((tm, bw), lambda i, j: (i, jnp.clip(j - nq, 0, 2 * nq - 1))))
        out_shape.append(jax.ShapeDtypeStruct((n, 2 * attn_w), F32))
    else:
        n_seq, sl = seqs
        assert n == n_seq * sl and sl % tm == 0 and nq == N_GROUPS
        tpb = sl // tm
        lws = tuple(min(w, sl) for w in ATTN_WINDOWS)
        for gi, lw in enumerate(lws):
            assert lw % tm == 0 or tm % lw == 0
            nt = max(lw // tm, 1)

            def kv_map(i, j, gi=gi, nt=nt):
                ii = i % tpb
                inside = ii >= tpb - nt
                spare = n_seq + (i // tpb) * (tpb - nt) + ii
                return (jnp.where(inside, i // tpb, spare), jnp.where(j < 2 * nq + gi, 0, 1),
                        jnp.where(inside, ii - (tpb - nt), 0))
            out_specs.append(pl.BlockSpec((None, bw, min(lw, tm)), kv_map))
            out_shape.append(jax.ShapeDtypeStruct(((1 + tpb - nt) * n_seq, 2 * bw, lw), F32))
    return pl.pallas_call(
        functools.partial(_in_proj_kernel, nq=nq, tpb=tpb, lws=lws),
        grid=(pl.cdiv(n, tm), nj),
        in_specs=[
            pl.BlockSpec((tm, d), row),
            pl.BlockSpec((1, d), lambda i, j: (0, 0)),
            pl.BlockSpec((d, bw), lambda i, j: (0, j)),
            pl.BlockSpec((tm, LANES), row),
            pl.BlockSpec((tm, LANES), row),
            pl.BlockSpec((tm, LANES), row),
            pl.BlockSpec((1, bw), lambda i, j: (0, 0)),
            pl.BlockSpec((1, bw), lambda i, j: (0, 0)),
        ],
        out_specs=out_specs,
        out_shape=out_shape,
        scratch_shapes=[pltpu.VMEM((tm, d), BF16)],
        compiler_params=_cparams(("arbitrary", "arbitrary")),
        name="in_proj",
    )(x, g, w_bf16, cos_t, sa_t, sb_t, ln_g, ln_b)


def _rope_tables(pos):
    half = ROT_DIM // 2
    inv = ROPE_THETA ** (-jnp.arange(0, ROT_DIM, 2, dtype=F32) / ROT_DIM)
    ang = pos.astype(F32)[:, None] * inv[None, :]
    cos, sin = jnp.cos(ang), jnp.sin(ang)
    rows = pos.shape[0]
    one = jnp.ones((rows, HEAD_DIM - ROT_DIM), F32)
    zero = jnp.zeros((rows, HEAD_DIM - ROT_DIM), F32)
    zh = jnp.zeros((rows, half), F32)
    cos_h = jnp.concatenate([cos, cos, one], axis=1)
    sa_h = jnp.concatenate([zh, sin, zero], axis=1)
    sb_h = jnp.concatenate([-sin, zh, zero], axis=1)
    rep = LANES // HEAD_DIM
    return tuple(jnp.tile(t, (1, rep)) for t in (cos_h, sa_h, sb_h))


def _attn_prompt_kernel(*refs, nk, dil, has_prev):
    if has_prev:
        q_ref, kp_ref, kc_ref, vp_ref, vc_ref, o_ref, lse_ref = refs
    else:
        q_ref, kc_ref, vc_ref, o_ref, lse_ref = refs
    n = pl.program_id(1)
    tq = ATTN_BLOCK
    tk = 2 * tq if has_prev else tq
    qi = lax.broadcasted_iota(jnp.int32, (tq, tk), 0)
    kj = lax.broadcasted_iota(jnp.int32, (tq, tk), 1)
    if has_prev:
        dist = qi + tq - kj
        valid = (dist >= 0) & (dist <= nk) & ((n - 1) * tq + kj >= 0)
    else:
        dist = qi - kj
        valid = (dist >= 0) & (dist <= nk)
    hph = q_ref.shape[0] * LANES // HEAD_DIM

    def residue(r, carry):
        rows = pl.ds(r, tq, stride=dil) if dil > 1 else pl.ds(0, tq)
        q2 = _unpack_bf16_pairs(_load_slabs(q_ref, rows))
        k2 = _unpack_bf16_pairs(_load_slabs(kc_ref, rows))
        v2 = _unpack_bf16_pairs(_load_slabs(vc_ref, rows))
        if has_prev:
            kp2 = _unpack_bf16_pairs(_load_slabs(kp_ref, rows))
            vp2 = _unpack_bf16_pairs(_load_slabs(vp_ref, rows))
            k2 = [jnp.concatenate([a, b], axis=0) for a, b in zip(kp2, k2)]
            v2 = [jnp.concatenate([a, b], axis=0) for a, b in zip(vp2, v2)]
        outs, lses = [], []
        for h in range(2 * hph):
            t, c0 = divmod(h, hph)
            sl = slice(c0 * HEAD_DIM, (c0 + 1) * HEAD_DIM)
            s = lax.dot_general(q2[t][:, sl], k2[t][:, sl], NT_DIMS,
                                preferred_element_type=F32) * (HEAD_DIM ** -0.5)
            s = jnp.where(valid, s, -jnp.inf)
            m = jnp.max(s, axis=-1, keepdims=True)
            p = jnp.exp(s - m)
            den = jnp.sum(p, axis=-1, keepdims=True)
            outs.append(jnp.dot(p.astype(BF16), v2[t][:, sl], preferred_element_type=F32) / den)
            lses.append(jnp.broadcast_to(m + jnp.log(den), (tq, HEAD_DIM)))
        o = jnp.concatenate(outs, axis=1)
        lse = jnp.concatenate(lses, axis=1)
        for c in range(o_ref.shape[0]):
            o_ref[c, rows, :] = o[:, c * LANES:(c + 1) * LANES]
            lse_ref[c, rows, :] = lse[:, c * LANES:(c + 1) * LANES]
        return carry

    if dil > 1:
        lax.fori_loop(0, dil, residue, 0)
    else:
        residue(0, 0)


def _attn_prompt(qkvp, g, *, bp, sp, gw):
    dil, win = ATTN_DILATIONS[g], ATTN_WINDOWS[g]
    nk = win // dil
    tile = ATTN_BLOCK * dil
    assert nk <= ATTN_BLOCK and sp % tile == 0
    nb = sp // tile
    has_prev = nb > 1
    blk = (gw // 2 // LANES, tile, LANES)
    cur = lambda col: pl.BlockSpec(blk, lambda b, n: (col, b * nb + n, 0))
    prev = lambda col: pl.BlockSpec(blk, lambda b, n: (col, b * nb + jnp.maximum(n - 1, 0), 0))
    kc, vc = N_GROUPS + g, 2 * N_GROUPS + g
    in_specs = [cur(g), prev(kc), cur(kc), prev(vc), cur(vc)] if has_prev else [cur(g), cur(kc), cur(vc)]
    out_spec = pl.BlockSpec((gw // LANES, tile, LANES), lambda b, n: (0, b * nb + n, 0))
    return pl.pallas_call(
        functools.partial(_attn_prompt_kernel, nk=nk, dil=dil, has_prev=has_prev),
        grid=(bp, nb),
        in_specs=in_specs,
        out_specs=[out_spec, out_spec],
        out_shape=[jax.ShapeDtypeStruct((gw // LANES, bp * sp, LANES), F32)] * 2,
        compiler_params=_cparams(("arbitrary", "arbitrary")),
        name=f"attn_prompt_g{g}",
    )(*([qkvp] * len(in_specs)))


def _attn_sample_kernel(q_ref, kvn_ref, s0_ref, s1_ref, s2_ref, o_ref, lse_ref, *, attn_w, gw):
    rows = 16
    ri = lax.broadcasted_iota(jnp.int32, (rows, gw), 0)
    lh = lax.broadcasted_iota(jnp.int32, (rows, gw), 1) // HEAD_DIM
    diag = ri == lh
    scale = HEAD_DIM ** -0.5
    kvn = kvn_ref[...]
    hw = gw // 2
    d = functools.partial(jnp.dot, preferred_element_type=F32)
    dn = functools.partial(lax.dot_general, dimension_numbers=NT_DIMS, preferred_element_type=F32)
    o_parts, lse_parts = [], []
    for g, s_ref in enumerate((s0_ref, s1_ref, s2_ref)):
        dil = ATTN_DILATIONS[g]
        nk = ATTN_WINDOWS[g] // dil
        lb = s_ref.shape[1]
        q = jnp.concatenate(_unpack_bf16_pairs(q_ref[:, g * hw:(g + 1) * hw]), axis=1).astype(F32)
        k_new = kvn[:, g * gw:(g + 1) * gw]
        v_new = kvn[:, attn_w + g * gw:attn_w + (g + 1) * gw]
        qblk = jnp.where(diag, jnp.broadcast_to(q, (rows, gw)), 0.0)
        q1, q2 = _split2(qblk)
        k1, k2 = _split2(s_ref[:gw, :])
        s = (d(q1, k1) + d(q2, k1) + d(q1, k2)) * scale
        back = lb - lax.broadcasted_iota(jnp.int32, (rows, lb), 1)
        valid = ((back & (dil - 1)) == 0) & (back <= dil * nk)
        s = jnp.where(valid, s, -jnp.inf)
        s_new = jnp.sum(qblk * k_new, axis=1, keepdims=True) * scale
        m = jnp.maximum(jnp.max(s, axis=1, keepdims=True), s_new)
        p = jnp.exp(s - m)
        p_new = jnp.exp(s_new - m)
        den = jnp.sum(p, axis=1, keepdims=True) + p_new
        lse = m + jnp.log(den)
        p1, p2 = _split2(p)
        v1, v2 = _split2(s_ref[gw:, :])
        o_full = (dn(p1, v1) + dn(p2, v1) + dn(p1, v2) + p_new * v_new) / den
        o_parts.append(jnp.sum(jnp.where(diag, o_full, 0.0), axis=0, keepdims=True))
        lse_parts.append(jnp.sum(jnp.where(diag, jnp.broadcast_to(lse, (rows, gw)), 0.0), axis=0, keepdims=True))
    o_ref[...] = jnp.concatenate(o_parts, axis=1)
    lse_ref[...] = jnp.concatenate(lse_parts, axis=1)


def _attn_sample(qkvp_s, kvf_s, states, *, attn_w, gw):
    bs = qkvp_s.shape[0]
    views, specs = [], []
    for g, st in enumerate(states):
        dil, win = ATTN_DILATIONS[g], ATTN_WINDOWS[g]
        lb = st.shape[1]
        assert lb == win and (dil & (dil - 1)) == 0 and st.shape[0] == bs
        views.append(jnp.transpose(st, (0, 2, 3, 4, 1)).reshape(bs, 2 * gw, lb))
        specs.append(pl.BlockSpec((None, 2 * gw, lb), lambda b: (b, 0, 0)))
    row3 = lambda w: pl.BlockSpec((None, 1, w), lambda b: (b, 0, 0))
    o, lse = pl.pallas_call(
        functools.partial(_attn_sample_kernel, attn_w=attn_w, gw=gw),
        grid=(bs,),
        in_specs=[row3(qkvp_s.shape[1]), row3(2 * attn_w), *specs],
        out_specs=[row3(attn_w), row3(attn_w)],
        out_shape=[jax.ShapeDtypeStruct((bs, 1, attn_w), F32),
                   jax.ShapeDtypeStruct((bs, 1, attn_w), F32)],
        compiler_params=_cparams(("arbitrary",)),
        name="attn_sample",
    )(qkvp_s.reshape(bs, 1, -1), kvf_s.reshape(bs, 1, 2 * attn_w), *views)
    return o.reshape(bs, attn_w), lse.reshape(bs, attn_w)


def _sgu_kernel(u_ref, va_ref, w_ref, bias_ref, g_ref, o_ref, *, cps):
    ng = w_ref.shape[0]
    ti = lax.broadcasted_iota(jnp.int32, (CHUNK, CHUNK), 0)
    si = lax.broadcasted_iota(jnp.int32, (CHUNK, CHUNK), 1)
    wt = [jnp.where(ti >= si, w_ref[g], 0.0).astype(BF16) for g in range(ng)]
    cw = va_ref.shape[1] // ng
    for c in range(cps):
        rows = slice(c * CHUNK, (c + 1) * CHUNK)
        va = va_ref[rows, :].astype(BF16)
        f = jnp.concatenate(
            [jnp.dot(wt[g], va[:, g * cw:(g + 1) * cw], preferred_element_type=F32) for g in range(ng)],
            axis=1) + bias_ref[...]
        s = u_ref[rows, :].astype(F32) * f
        ms = jnp.mean(s * s, axis=-1, keepdims=True)
        o_ref[rows, :] = (s * lax.rsqrt(ms + EPS) * g_ref[...]).astype(BF16)


def _sgu_prompt(u, va, w_sp, bias_tc, g_a, *, cps):
    n, w = u.shape
    tm = cps * CHUNK
    assert n % tm == 0
    row = lambda i: (i, 0)
    return pl.pallas_call(
        functools.partial(_sgu_kernel, cps=cps),
        grid=(n // tm,),
        in_specs=[pl.BlockSpec((tm, w), row), pl.BlockSpec((tm, w), row),
                  pl.BlockSpec(w_sp.shape, lambda i: (0, 0, 0)),
                  pl.BlockSpec((CHUNK, w), lambda i: (0, 0)),
                  pl.BlockSpec((1, w), lambda i: (0, 0))],
        out_specs=pl.BlockSpec((tm, w), row),
        out_shape=jax.ShapeDtypeStruct((n, w), BF16),
        compiler_params=_cparams(("arbitrary",)),
        name="sgu_prompt",
    )(u, va, w_sp, bias_tc, g_a)


def _sgu_sample_kernel(u_ref, va_ref, w0_ref, b0_ref, g_ref, o_ref):
    s = u_ref[...].astype(F32) * (va_ref[...] * w0_ref[...] + b0_ref[...])
    ms = jnp.mean(s * s, axis=-1, keepdims=True)
    o_ref[...] = (s * lax.rsqrt(ms + EPS) * g_ref[...]).astype(BF16)


def _sgu_sample(u, va, w0, b0, g_a):
    n, w = u.shape
    full = pl.BlockSpec((n, w), lambda i: (0, 0))
    vec = pl.BlockSpec((1, w), lambda i: (0, 0))
    return pl.pallas_call(
        _sgu_sample_kernel, grid=(1,),
        in_specs=[full, full, vec, vec, vec], out_specs=full,
        out_shape=jax.ShapeDtypeStruct((n, w), BF16),
        name="sgu_sample",
    )(u, va, w0, b0, g_a)


def _out_proj_kernel(*refs, aliased):
    ng = N_GROUPS
    o_refs, lse_refs = refs[:ng], refs[ng:2 * ng]
    rest = refs[2 * ng:]
    if aliased:
        rest = rest[:7] + rest[10:]
    sgu_ref, x_ref, gb_ref, wout_ref, gffn_ref, wr_ref, br_ref, x2_ref, xnp_ref, lg_ref = rest
    ls = [_load_slabs(r) for r in lse_refs]
    m = functools.reduce(jnp.maximum, ls)
    es = [jnp.exp(l - m) for l in ls]
    inv = 1.0 / functools.reduce(jnp.add, es)
    attn = [_load_slabs(o_refs[g]) * (es[g] * inv) for g in range(ng)]
    gw = attn[0].shape[1]
    ms = functools.reduce(jnp.add, [jnp.sum(a * a, axis=-1, keepdims=True) for a in attn]) / (ng * gw)
    rstd = lax.rsqrt(ms + EPS)
    y = jnp.dot(sgu_ref[...], wout_ref[ng * gw:, :], preferred_element_type=F32)
    for g in range(ng):
        a_n = (attn[g] * rstd * gb_ref[:, g * gw:(g + 1) * gw]).astype(BF16)
        y = y + jnp.dot(a_n, wout_ref[g * gw:(g + 1) * gw, :], preferred_element_type=F32)
    x2 = x_ref[...] + y
    x2_ref[...] = x2
    ms2 = jnp.mean(x2 * x2, axis=-1, keepdims=True)
    xn = x2 * lax.rsqrt(ms2 + EPS) * gffn_ref[...]
    x1, xr = _split2(xn)
    d = functools.partial(jnp.dot, preferred_element_type=F32)
    lg_ref[...] = d(x1, wr_ref[0]) + d(x1, wr_ref[1]) + d(xr, wr_ref[0]) + br_ref[...]
    xnp_ref[...] = _pack_bf16_pairs(xn)


def _out_proj(o_l, lse_l, sgu_n, x, g_b, wout_bf16, g_ffn, wr_hl, b_r, *, tm, n_total, row0, prev=None):
    n, d = x.shape
    gw = o_l[0].shape[0] * LANES
    assert n % tm == 0 and row0 % tm == 0 and len(o_l) == N_GROUPS
    off = row0 // tm
    row = lambda i: (i, 0)
    orow = lambda i: (i + off, 0)
    const2 = lambda i: (0, 0)
    in_specs = [pl.BlockSpec((gw // LANES, tm, LANES), lambda i: (0, i, 0))] * (2 * N_GROUPS) + [
        pl.BlockSpec((tm, sgu_n.shape[1]), row),
        pl.BlockSpec((tm, d), row), pl.BlockSpec((1, N_GROUPS * gw), const2),
        pl.BlockSpec((d, d), const2), pl.BlockSpec((1, d), const2),
        pl.BlockSpec((2, d, LANES), lambda i: (0, 0, 0)), pl.BlockSpec((1, LANES), const2),
    ]
    args = [*o_l, *lse_l, sgu_n, x, g_b, wout_bf16, g_ffn, wr_hl, b_r]
    aliases = {}
    if prev is not None:
        in_specs += [pl.BlockSpec(memory_space=pl.ANY)] * 3
        aliases = {len(args) + k: k for k in range(3)}
        args += list(prev)
    return pl.pallas_call(
        functools.partial(_out_proj_kernel, aliased=prev is not None),
        grid=(n // tm,),
        in_specs=in_specs,
        out_specs=[pl.BlockSpec((tm, d), orow), pl.BlockSpec((tm, d // 2), orow),
                   pl.BlockSpec((tm, LANES), orow)],
        out_shape=[jax.ShapeDtypeStruct((n_total, d), F32),
                   jax.ShapeDtypeStruct((n_total, d // 2), U32),
                   jax.ShapeDtypeStruct((n_total, LANES), F32)],
        input_output_aliases=aliases,
        compiler_params=_cparams(("arbitrary",)),
        name="out_proj",
    )(*args)


def _moe_kernel(be_ref, ns_ref, sb_ref, pro_ref, pref_ref, dst_ref, gate_ref, xn_hbm,
                wg_ref, wl_ref, bg_ref, bl_ref, wd_ref, bd_ref, out_hbm,
                xb_sc, acc_sc, wg_sc, wl_sc, wd_sc, sem_in, sem_out, *, nc):
    b = pl.program_id(0)
    c = pl.program_id(1)
    nb = pl.num_programs(0)
    ns = ns_ref[b]
    s0 = sb_ref[b]
    del be_ref
    per_it = xb_sc.shape[1]
    half = xb_sc.shape[2]
    d = functools.partial(jnp.dot, preferred_element_type=F32)

    def row_in(tok, slot, part, k):
        return pltpu.make_async_copy(xn_hbm.at[pl.ds(tok, 1), :], xb_sc.at[slot * nc + part, pl.ds(k, 1), :],
                                     sem_in.at[slot])

    def sub_in_wait(slot):
        for _ in range(nc):
            pltpu.make_async_copy(xn_hbm.at[pl.ds(0, per_it), :], xb_sc.at[0], sem_in.at[slot]).wait()

    def sub_out_wait():
        pltpu.make_async_copy(acc_sc.at[0], out_hbm.at[pl.ds(0, MOE_SUB), :], sem_out).wait()

    def rows_of(s):
        return pl.ds(pl.multiple_of(s * MOE_SUB, MOE_SUB), MOE_SUB)

    def compute(s):
        it = c * ns + s
        slot = (s0 + MOE_NSUB + it // nc) % MOE_RING
        part = it % nc
        for k in range(per_it):
            row_in(pref_ref[0, it * per_it + k], slot, part, k).start()
        x0 = ((s0 + s) % MOE_RING) * nc
        x_hi, x_lo = _unpack_bf16_pairs(xb_sc[pl.ds(x0, nc)].reshape(MOE_SUB, half))
        hg = d(x_hi, wg_sc[:half, :]) + d(x_lo, wg_sc[half:, :]) + bg_ref[...]
        hl = d(x_hi, wl_sc[:half, :]) + d(x_lo, wl_sc[half:, :]) + bl_ref[...]
        glu = jnp.minimum(hg, SWIGLU_LIMIT)
        lin = jnp.clip(hl, -SWIGLU_LIMIT, SWIGLU_LIMIT)
        a = glu * jax.nn.sigmoid(SWIGLU_ALPHA * glu) * (lin + 1.0)
        return d(a.astype(BF16), wd_sc[...])

    def scatter(s):
        a0 = s * MOE_SUB
        for k in range(MOE_SUB):
            pltpu.make_async_copy(acc_sc.at[s, pl.ds(k, 1), :],
                                  out_hbm.at[pl.ds(dst_ref[0, a0 + k], 1), :], sem_out).start()

    def out_wait(n_sub):
        def wait(s, carry):
            sub_out_wait()
            return carry
        lax.fori_loop(0, n_sub, wait, 0)

    @pl.when((b == 0) & (c == 0))
    def _():
        def start(i, carry):
            part = i // per_it
            pltpu.make_async_copy(xn_hbm.at[pl.ds(pro_ref[0, i], 1), :], xb_sc.at[part, pl.ds(i % per_it, 1), :],
                                  sem_in.at[part // nc]).start()
            return carry
        lax.fori_loop(0, MOE_TM, start, 0)

    @pl.when((b > 0) & (c == 0))
    def _():
        out_wait(ns_ref[b - 1])

    @pl.when(ns > 0)
    def _():
        @pl.when(c == 0)
        def _():
            for s in range(MOE_NSUB):
                @pl.when(s < ns)
                def _():
                    sub_in_wait((s0 + s) % MOE_RING)

        wg_sc[...] = wg_ref[...].astype(BF16)
        wl_sc[...] = wl_ref[...].astype(BF16)
        wd_sc[...] = wd_ref[...].astype(BF16)

        @pl.when(c == 0)
        def _():
            def body(s, carry):
                acc_sc[s] = compute(s) + bd_ref[...]
                return carry
            lax.fori_loop(0, ns, body, 0)

        @pl.when((c > 0) & (c < nc - 1))
        def _():
            def body(s, carry):
                acc_sc[s] += compute(s)
                return carry
            lax.fori_loop(0, ns, body, 0)

        @pl.when(c == nc - 1)
        def _():
            def finish(s):
                acc_sc[s] = (acc_sc[s] + compute(s)) * gate_ref[rows_of(s), :]

            finish(0)

            def body(s, carry):
                finish(s)
                scatter(s - 1)
                return carry
            lax.fori_loop(1, ns, body, 0)
            scatter(ns - 1)

            @pl.when(b == nb - 1)
            def _():
                out_wait(ns)

    @pl.when((b == nb - 1) & (c == nc - 1))
    def _():
        total = sb_ref[nb - 1] + ns_ref[nb - 1]
        for k in range(MOE_NSUB):
            sub_in_wait((total + k) % MOE_RING)


def _moe(xnp, sched, w_up, b_up, w_down, b_down, *, n_rows_out):
    be, nsub, sub0, pro_tbl, pref_tbl, dst_tbl, gate_tbl = sched
    n_tok, half = xnp.shape
    d = 2 * half
    n_exp, _, ff2 = w_up.shape
    ff = ff2 // 2
    assert ff % MOE_TF == 0 and MOE_TF % LANES == 0
    nc = ff // MOE_TF
    assert nc > 2 and MOE_SUB % (8 * nc) == 0
    nbmax = be.shape[0]

    def cidx(b, c, ns):
        return jnp.where(ns[b] > 0, c, nc - 1)

    smem_tbl = lambda: pl.BlockSpec((None, 1, MOE_TM), lambda b, c, be, ns, sb: (b, 0, 0), memory_space=pltpu.SMEM)
    grid_spec = pltpu.PrefetchScalarGridSpec(
        num_scalar_prefetch=3,
        grid=(nbmax, nc),
        in_specs=[
            pl.BlockSpec((1, MOE_TM), lambda b, c, be, ns, sb: (0, 0), memory_space=pltpu.SMEM),
            smem_tbl(),
            smem_tbl(),
            pl.BlockSpec((None, MOE_TM, 1), lambda b, c, be, ns, sb: (b, 0, 0)),
            pl.BlockSpec(memory_space=pl.ANY),
            pl.BlockSpec((None, d, MOE_TF), lambda b, c, be, ns, sb: (be[b], 0, cidx(b, c, ns))),
            pl.BlockSpec((None, d, MOE_TF), lambda b, c, be, ns, sb: (be[b], 0, nc + cidx(b, c, ns))),
            pl.BlockSpec((None, 1, MOE_TF), lambda b, c, be, ns, sb: (be[b], 0, cidx(b, c, ns))),
            pl.BlockSpec((None, 1, MOE_TF), lambda b, c, be, ns, sb: (be[b], 0, nc + cidx(b, c, ns))),
            pl.BlockSpec((None, MOE_TF, d), lambda b, c, be, ns, sb: (be[b], cidx(b, c, ns), 0)),
            pl.BlockSpec((None, 1, d), lambda b, c, be, ns, sb: (be[b], 0, 0)),
        ],
        out_specs=pl.BlockSpec(memory_space=pl.ANY),
        scratch_shapes=[
            pltpu.VMEM((MOE_RING * nc, MOE_SUB // nc, half), U32),
            pltpu.VMEM((MOE_NSUB, MOE_SUB, d), F32),
            pltpu.VMEM((d, MOE_TF), BF16),
            pltpu.VMEM((d, MOE_TF), BF16),
            pltpu.VMEM((MOE_TF, d), BF16),
            pltpu.SemaphoreType.DMA((MOE_RING,)),
            pltpu.SemaphoreType.DMA,
        ],
    )
    return pl.pallas_call(
        functools.partial(_moe_kernel, nc=nc),
        grid_spec=grid_spec,
        out_shape=jax.ShapeDtypeStruct((n_rows_out, d), F32),
        compiler_params=_cparams(("arbitrary", "arbitrary")),
        name="moe",
    )(be, nsub, sub0, pro_tbl, pref_tbl, dst_tbl, gate_tbl, xnp, w_up, w_up,
      b_up.reshape(n_exp, 1, ff2), b_up.reshape(n_exp, 1, ff2), w_down, b_down.reshape(n_exp, 1, d))


def _moe_schedule(logits, n_exp, n_pad):
    n = logits.shape[0]
    i32 = jnp.int32
    top_v, top_i = lax.top_k(logits, TOP_K)
    gates = jax.nn.softmax(top_v, axis=-1)
    na = n * TOP_K
    flat_e = top_i.reshape(-1).astype(i32)
    flat_g = gates.reshape(-1)
    order = jnp.argsort(flat_e, stable=True).astype(i32)
    counts = jnp.bincount(flat_e, length=n_exp).astype(i32)
    starts = jnp.cumsum(counts) - counts
    ns_e = (counts + MOE_SUB - 1) // MOE_SUB
    nblk_e = (ns_e + MOE_NSUB - 1) // MOE_NSUB
    spb_e = -(-ns_e // jnp.maximum(nblk_e, 1))
    sub_ends = jnp.cumsum(ns_e)
    sub_starts = sub_ends - ns_e
    total_subs = sub_ends[-1]
    bends = jnp.cumsum(nblk_e)
    total_blocks = bends[-1]

    nsub_max = -(-na // MOE_SUB) + n_exp
    nbmax = -(-nsub_max // MOE_NSUB) + n_exp
    bid = jnp.arange(nbmax, dtype=i32)
    used = bid < total_blocks
    owner = lambda ids, ends: jnp.minimum(jnp.sum(ids[:, None] >= ends[None, :], axis=1), n_exp - 1).astype(i32)
    e_b = owner(jnp.minimum(bid, total_blocks - 1), bends)
    jb = bid - (bends[e_b] - nblk_e[e_b])
    sub0 = jnp.where(used, sub_starts[e_b] + jb * spb_e[e_b], total_subs).astype(i32)
    nsub = jnp.where(used, jnp.clip(ns_e[e_b] - jb * spb_e[e_b], 0, spb_e[e_b]), 0).astype(i32)

    gs = nsub_max + 2 * MOE_NSUB
    gid = jnp.arange(gs, dtype=i32)
    e_g = owner(gid, sub_ends)
    lane = jnp.arange(MOE_SUB, dtype=i32)[None, :]
    within = (gid - sub_starts[e_g])[:, None] * MOE_SUB + lane
    valid = (gid < total_subs)[:, None] & (within < counts[e_g][:, None])
    a = order[jnp.clip(starts[e_g][:, None] + within, 0, na - 1)]
    tok_f = jnp.where(valid, a // TOP_K, 0).astype(i32).reshape(-1)
    dst_f = jnp.where(valid, (a % TOP_K) * n_pad + a // TOP_K, n + lane).astype(i32).reshape(-1)
    gate_f = jnp.where(valid, flat_g[a], 0.0).astype(F32).reshape(-1)

    own = sub0[:, None] * MOE_SUB + jnp.arange(MOE_TM, dtype=i32)[None, :]
    dst_tbl = dst_f[own].reshape(nbmax, 1, MOE_TM)
    gate_tbl = gate_f[own].reshape(nbmax, MOE_TM, 1)
    pref_tbl = tok_f[own + MOE_TM].reshape(nbmax, 1, MOE_TM)
    pro_tbl = tok_f[:MOE_TM].reshape(1, MOE_TM)
    return e_b, nsub, sub0, pro_tbl, pref_tbl, dst_tbl, gate_tbl


def _final_kernel(x2_ref, y4_ref, g_ref, o_ref):
    y = y4_ref[0]
    for k in range(1, TOP_K):
        y = y + y4_ref[k]
    x = x2_ref[...] + y
    ms = jnp.mean(x * x, axis=-1, keepdims=True)
    o_ref[...] = x * lax.rsqrt(ms + EPS) * g_ref[...]


def _final(x2, y4, g, *, tm, row0, n):
    d = x2.shape[1]
    assert n % tm == 0 and row0 % tm == 0
    off = row0 // tm
    return pl.pallas_call(
        _final_kernel, grid=(n // tm,),
        in_specs=[pl.BlockSpec((tm, d), lambda i: (i + off, 0)),
                  pl.BlockSpec((TOP_K, tm, d), lambda i: (0, i + off, 0)),
                  pl.BlockSpec((1, d), lambda i: (0, 0))],
        out_specs=pl.BlockSpec((tm, d), lambda i: (i, 0)),
        out_shape=jax.ShapeDtypeStruct((n, d), F32),
        compiler_params=_cparams(("arbitrary",)),
        name="final",
    )(x2, y4, g)


def kernel(x_prompt, x_sample, state_b0_kv, state_b1_kv, state_b2_kv, norm_mix_g, w_in, ln_va_g, ln_va_b,
           w_spatial, b_spatial, norm_out_b_g, norm_out_a_g, w_out, norm_ffn_g, w_router, b_router,
           w_up, b_up, w_down, b_down, norm_final_g):
    bp, sp, d = x_prompt.shape
    bs, ss, _ = x_sample.shape
    depth = w_in.shape[0]
    assert depth == 1 and ss == 1
    hpg = state_b0_kv.shape[4]
    gw = hpg * HEAD_DIM
    attn_w = N_GROUPS * gw
    sgu_w = ln_va_g.shape[1]
    n_exp = w_router.shape[2]
    npr, nsm = bp * sp, bs * ss
    n_total = npr + nsm
    states = (state_b0_kv[0], state_b1_kv[0], state_b2_kv[0])

    w_in_b = w_in[0].astype(BF16)
    w_out_b = w_out[0].astype(BF16)
    g_mix = norm_mix_g[0].reshape(1, d)
    ln_g, ln_b = ln_va_g[0].reshape(1, sgu_w), ln_va_b[0].reshape(1, sgu_w)
    g_b = norm_out_b_g[0].reshape(1, attn_w)
    g_a = norm_out_a_g[0].reshape(1, sgu_w)
    g_ffn = norm_ffn_g[0].reshape(1, d)
    ng = w_spatial.shape[1]
    cw = sgu_w // ng
    bias_tc = jnp.repeat(b_spatial[0].T, cw, axis=1)
    w00 = jnp.repeat(w_spatial[0][:, 0, 0], cw).reshape(1, sgu_w)
    b00 = jnp.repeat(b_spatial[0][:, 0], cw).reshape(1, sgu_w)
    wr = jnp.pad(w_router[0], ((0, 0), (0, LANES - n_exp)))
    wr_hl = jnp.stack(_split2(wr), 0)
    b_r = jnp.pad(b_router[0], (0, LANES - n_exp)).reshape(1, LANES)

    tabs_p = _rope_tables(jnp.tile(jnp.arange(sp), bp))
    tabs_s = _rope_tables(jnp.tile(PAST_LEN + jnp.arange(ss), bs))

    xp = x_prompt.reshape(npr, d)
    xs = x_sample.reshape(nsm, d)
    qkvp_p, u_p, va_p, *kvt_p = _in_proj(xp, g_mix, w_in_b, *tabs_p, ln_g, ln_b, tm=1024,
                                         attn_w=attn_w, sgu_w=sgu_w, seqs=(bp, sp))
    qkvp_s, u_s, va_s, kvf_s = _in_proj(xs, g_mix, w_in_b, *tabs_s, ln_g, ln_b, tm=nsm,
                                        attn_w=attn_w, sgu_w=sgu_w)

    o_p, lse_p = zip(*[_attn_prompt(qkvp_p, g, bp=bp, sp=sp, gw=gw) for g in range(N_GROUPS)])
    rows_s = jnp.transpose(qkvp_s, (1, 0, 2)).reshape(nsm, -1)
    o_s, lse_s = _attn_sample(rows_s, kvf_s, states, attn_w=attn_w, gw=gw)
    split = lambda t: [jnp.transpose(t[:, g * gw:(g + 1) * gw].reshape(nsm, gw // LANES, LANES), (1, 0, 2))
                       for g in range(N_GROUPS)]

    sgu_p = _sgu_prompt(u_p, va_p, w_spatial[0], bias_tc, g_a, cps=4)
    sgu_s = _sgu_sample(u_s, va_s, w00, b00, g_a)

    prev = _out_proj(o_p, lse_p, sgu_p, xp, g_b, w_out_b, g_ffn, wr_hl, b_r, tm=256, n_total=n_total, row0=0)
    x2, xnp, logits = _out_proj(split(o_s), split(lse_s), sgu_s, xs, g_b, w_out_b, g_ffn, wr_hl, b_r,
                                tm=nsm, n_total=n_total, row0=npr, prev=prev)

    n_pad = n_total + MOE_SUB
    assert n_pad % 8 == 0
    sched = _moe_schedule(logits[:, :n_exp], n_exp, n_pad)
    y4 = _moe(xnp, sched, w_up[0], b_up[0], w_down[0], b_down[0], n_rows_out=TOP_K * n_pad)
    y4 = y4.reshape(TOP_K, n_pad, d)
    g_fin = norm_final_g.reshape(1, d)
    y_p = _final(x2, y4, g_fin, tm=256, row0=0, n=npr).reshape(bp, sp, d)
    y_s = _final(x2, y4, g_fin, tm=nsm, row0=npr, n=nsm).reshape(bs, ss, d)

    kv_prompt = [jnp.transpose(t[:bp].reshape(bp, 2, hpg, HEAD_DIM, t.shape[2]), (0, 4, 1, 2, 3))[None]
                 for t in kvt_p]
    ksm = kvf_s.reshape(bs, ss, 2, N_GROUPS, hpg, HEAD_DIM)
    kv_sample = [ksm[:, :, :, g][None] for g in range(N_GROUPS)]
    v_sgu = va_s.reshape(1, bs, ss, sgu_w)
    return (y_p, y_s, *kv_prompt, *kv_sample, v_sgu)
```

```python
import functools

import jax
import jax.numpy as jnp
import numpy as np
from jax import lax
from jax.experimental import pallas as pl
from jax.experimental.pallas import tpu as pltpu

F32 = jnp.float32
BF16 = jnp.bfloat16

HEAD_DIM = 64
ROT_DIM = HEAD_DIM // 4
ROPE_THETA = 500000.0
ATTN_WINDOWS = (128, 512, 2048)
ATTN_DILATIONS = (1, 4, 16)
N_GROUPS = 3
CHUNK = 128
ATTN_BLOCK = 128
TOP_K = 4
SWIGLU_LIMIT = 7.0
SWIGLU_ALPHA = 1.702
EPS = 1e-5
PAST_LEN = 8192

LANES = 128
VMEM_LIMIT = 60 * 1024 * 1024

MOE_SUB = 256
MOE_NSUB = 4
MOE_TM = MOE_NSUB * MOE_SUB
MOE_RING = 2 * MOE_NSUB
MOE_TF = 512

NT_DIMS = (((1,), (1,)), ((), ()))


def _cparams(sem):
    return pltpu.CompilerParams(dimension_semantics=sem, vmem_limit_bytes=VMEM_LIMIT)


def _gelu(x):
    return 0.5 * x * (1.0 + lax.erf(x * np.float32(np.sqrt(0.5))))


def _split2(a):
    a1 = a.astype(BF16)
    return a1, (a - a1.astype(F32)).astype(BF16)


def _store_slabs(ref, x):
    for c in range(ref.shape[0]):
        ref[c] = x[:, c * LANES:(c + 1) * LANES]


def _load_slabs(ref, rows=slice(None)):
    return jnp.concatenate([ref[c, rows, :] for c in range(ref.shape[0])], axis=1)


def _in_proj_kernel(x_ref, g_ref, w_ref, cos_ref, sa_ref, sb_ref, lng_ref, lnb_ref,
                    qkv_ref, u_ref, va_ref, *rest, nq, tpb, lws):
    kv_refs, xn_sc = rest[:-1], rest[-1]
    i = pl.program_id(0)
    j = pl.program_id(1)
    tm = x_ref.shape[0]

    @pl.when(j == 0)
    def _():
        x = x_ref[...]
        ms = jnp.mean(x * x, axis=-1, keepdims=True)
        xn_sc[...] = (x * lax.rsqrt(ms + EPS) * g_ref[...]).astype(BF16)

    def matmul():
        return jnp.dot(xn_sc[...], w_ref[...], preferred_element_type=F32)

    def emit_kv(r):
        if lws is None:
            kv_refs[0][...] = r
            return
        ii = i % tpb
        for g, lw in enumerate(lws):
            @pl.when(((j == nq + g) | (j == 2 * nq + g)) & (ii >= tpb - max(lw // tm, 1)))
            def _():
                kv_refs[g][...] = r[tm - min(lw, tm):, :].T

    @pl.when(j < 2 * nq)
    def _():
        acc = matmul()
        cos, sa, sb = cos_ref[...], sa_ref[...], sb_ref[...]
        parts = []
        for s in range(acc.shape[1] // LANES):
            z = acc[:, s * LANES:(s + 1) * LANES]
            parts.append(z * cos + pltpu.roll(z, ROT_DIM // 2, 1) * sa
                         + pltpu.roll(z, LANES - ROT_DIM // 2, 1) * sb)
        r = jnp.concatenate(parts, axis=1)
        _store_slabs(qkv_ref, r)

        @pl.when(j >= nq)
        def _():
            emit_kv(r)

    @pl.when((j >= 2 * nq) & (j < 3 * nq))
    def _():
        acc = matmul()
        _store_slabs(qkv_ref, acc)
        emit_kv(acc)

    @pl.when(j == 3 * nq)
    def _():
        u_ref[...] = _gelu(matmul()).astype(BF16)

    @pl.when(j == 3 * nq + 1)
    def _():
        t = _gelu(matmul())
        mu = jnp.mean(t, axis=-1, keepdims=True)
        var = jnp.mean(jnp.square(t - mu), axis=-1, keepdims=True)
        va_ref[...] = (t - mu) * lax.rsqrt(var + EPS) * lng_ref[...] + lnb_ref[...]


def _in_proj(x, g, w_bf16, cos_t, sa_t, sb_t, ln_g, ln_b, *, tm, attn_w, sgu_w, seqs=None):
    n, d = x.shape
    bw = sgu_w
    nq = attn_w // bw
    nj = 3 * nq + 2
    assert w_bf16.shape[1] == nj * bw and bw % (2 * LANES) == 0
    row = lambda i, j: (i, 0)
    out_specs = [
        pl.BlockSpec((bw // LANES, tm, LANES), lambda i, j: (jnp.minimum(j, 3 * nq - 1), i, 0)),
        pl.BlockSpec((tm, bw), row),
        pl.BlockSpec((tm, bw), row),
    ]
    out_shape = [
        jax.ShapeDtypeStruct((3 * attn_w // LANES, n, LANES), F32),
        jax.ShapeDtypeStruct((n, sgu_w), BF16),
        jax.ShapeDtypeStruct((n, sgu_w), F32),
    ]
    if seqs is None:
        tpb, lws = 1, None
        out_specs.append(pl.BlockSpec((tm, bw), lambda i, j: (i, jnp.clip(j - nq, 0, 2 * nq - 1))))
        out_shape.append(jax.ShapeDtypeStruct((n, 2 * attn_w), F32))
    else:
        n_seq, sl = seqs
        assert n == n_seq * sl and sl % tm == 0 and nq == N_GROUPS
        tpb = sl // tm
        lws = tuple(min(w, sl) for w in ATTN_WINDOWS)
        for gi, lw in enumerate(lws):
            assert lw % tm == 0 or tm % lw == 0
            nt = max(lw // tm, 1)

            def kv_map(i, j, gi=gi, nt=nt):
                ii = i % tpb
                inside = ii >= tpb - nt
                spare = n_seq + (i // tpb) * (tpb - nt) + ii
                return (jnp.where(inside, i // tpb, spare), jnp.where(j < 2 * nq + gi, 0, 1),
                        jnp.where(inside, ii - (tpb - nt), 0))
            out_specs.append(pl.BlockSpec((None, bw, min(lw, tm)), kv_map))
            out_shape.append(jax.ShapeDtypeStruct(((1 + tpb - nt) * n_seq, 2 * bw, lw), F32))
    return pl.pallas_call(
        functools.partial(_in_proj_kernel, nq=nq, tpb=tpb, lws=lws),
        grid=(pl.cdiv(n, tm), nj),
        in_specs=[
            pl.BlockSpec((tm, d), row),
            pl.BlockSpec((1, d), lambda i, j: (0, 0)),
            pl.BlockSpec((d, bw), lambda i, j: (0, j)),
            pl.BlockSpec((tm, LANES), row),
            pl.BlockSpec((tm, LANES), row),
            pl.BlockSpec((tm, LANES), row),
            pl.BlockSpec((1, bw), lambda i, j: (0, 0)),
            pl.BlockSpec((1, bw), lambda i, j: (0, 0)),
        ],
        out_specs=out_specs,
        out_shape=out_shape,
        scratch_shapes=[pltpu.VMEM((tm, d), BF16)],
        compiler_params=_cparams(("arbitrary", "arbitrary")),
        name="in_proj",
    )(x, g, w_bf16, cos_t, sa_t, sb_t, ln_g, ln_b)


def _rope_tables(pos):
    half = ROT_DIM // 2
    inv = ROPE_THETA ** (-jnp.arange(0, ROT_DIM, 2, dtype=F32) / ROT_DIM)
    ang = pos.astype(F32)[:, None] * inv[None, :]
    cos, sin = jnp.cos(ang), jnp.sin(ang)
    rows = pos.shape[0]
    one = jnp.ones((rows, HEAD_DIM - ROT_DIM), F32)
    zero = jnp.zeros((rows, HEAD_DIM - ROT_DIM), F32)
    zh = jnp.zeros((rows, half), F32)
    cos_h = jnp.concatenate([cos, cos, one], axis=1)
    sa_h = jnp.concatenate([zh, sin, zero], axis=1)
    sb_h = jnp.concatenate([-sin, zh, zero], axis=1)
    rep = LANES // HEAD_DIM
    return tuple(jnp.tile(t, (1, rep)) for t in (cos_h, sa_h, sb_h))


def _attn_prompt_kernel(*refs, nk, dil, has_prev):
    if has_prev:
        q_ref, kp_ref, kc_ref, vp_ref, vc_ref, o_ref, lse_ref = refs
    else:
        q_ref, kc_ref, vc_ref, o_ref, lse_ref = refs
    n = pl.program_id(1)
    tq = ATTN_BLOCK
    tk = 2 * tq if has_prev else tq
    qi = lax.broadcasted_iota(jnp.int32, (tq, tk), 0)
    kj = lax.broadcasted_iota(jnp.int32, (tq, tk), 1)
    if has_prev:
        dist = qi + tq - kj
        valid = (dist >= 0) & (dist <= nk) & ((n - 1) * tq + kj >= 0)
    else:
        dist = qi - kj
        valid = (dist >= 0) & (dist <= nk)
    hpg = q_ref.shape[0] * LANES // HEAD_DIM
    head_lane = lax.broadcasted_iota(jnp.int32, (tq, LANES), 1)

    def residue(r, carry):
        rows = pl.ds(r, tq, stride=dil) if dil > 1 else pl.ds(0, tq)
        q = _load_slabs(q_ref, rows).astype(BF16)
        k = _load_slabs(kc_ref, rows).astype(BF16)
        v = _load_slabs(vc_ref, rows).astype(BF16)
        if has_prev:
            k = jnp.concatenate([_load_slabs(kp_ref, rows).astype(BF16), k], axis=0)
            v = jnp.concatenate([_load_slabs(vp_ref, rows).astype(BF16), v], axis=0)
        outs = []
        lse = jnp.zeros((tq, LANES), F32)
        for h in range(hpg):
            sl = slice(h * HEAD_DIM, (h + 1) * HEAD_DIM)
            s = lax.dot_general(q[:, sl], k[:, sl], NT_DIMS, preferred_element_type=F32) * (HEAD_DIM ** -0.5)
            s = jnp.where(valid, s, -jnp.inf)
            m = jnp.max(s, axis=-1, keepdims=True)
            p = jnp.exp(s - m)
            den = jnp.sum(p, axis=-1, keepdims=True)
            outs.append(jnp.dot(p.astype(BF16), v[:, sl], preferred_element_type=F32) / den)
            lse = jnp.where(head_lane == h, m + jnp.log(den), lse)
        o = jnp.concatenate(outs, axis=1)
        for c in range(o_ref.shape[0]):
            o_ref[c, rows, :] = o[:, c * LANES:(c + 1) * LANES]
        lse_ref[rows, :] = lse
        return carry

    if dil > 1:
        lax.fori_loop(0, dil, residue, 0)
    else:
        residue(0, 0)


def _attn_prompt(qkvp, g, *, bp, sp, gw):
    dil, win = ATTN_DILATIONS[g], ATTN_WINDOWS[g]
    nk = win // dil
    tile = ATTN_BLOCK * dil
    assert nk <= ATTN_BLOCK and sp % tile == 0
    nb = sp // tile
    has_prev = nb > 1
    blk = (gw // LANES, tile, LANES)
    cur = lambda col: pl.BlockSpec(blk, lambda b, n: (col, b * nb + n, 0))
    prev = lambda col: pl.BlockSpec(blk, lambda b, n: (col, b * nb + jnp.maximum(n - 1, 0), 0))
    kc, vc = N_GROUPS + g, 2 * N_GROUPS + g
    in_specs = [cur(g), prev(kc), cur(kc), prev(vc), cur(vc)] if has_prev else [cur(g), cur(kc), cur(vc)]
    out_spec = pl.BlockSpec((gw // LANES, tile, LANES), lambda b, n: (0, b * nb + n, 0))
    return pl.pallas_call(
        functools.partial(_attn_prompt_kernel, nk=nk, dil=dil, has_prev=has_prev),
        grid=(bp, nb),
        in_specs=in_specs,
        out_specs=[out_spec, pl.BlockSpec((tile, LANES), lambda b, n: (b * nb + n, 0))],
        out_shape=[jax.ShapeDtypeStruct((gw // LANES, bp * sp, LANES), F32),
                   jax.ShapeDtypeStruct((bp * sp, LANES), F32)],
        compiler_params=_cparams(("arbitrary", "arbitrary")),
        name=f"attn_prompt_g{g}",
    )(*([qkvp] * len(in_specs)))


def _attn_sample_kernel(q_ref, kvn_ref, s0_ref, s1_ref, s2_ref, o_ref, lse_ref, *, attn_w, gw):
    rows = 16
    ri = lax.broadcasted_iota(jnp.int32, (rows, gw), 0)
    lh = lax.broadcasted_iota(jnp.int32, (rows, gw), 1) // HEAD_DIM
    diag = ri == lh
    scale = HEAD_DIM ** -0.5
    kvn = kvn_ref[...]
    d = functools.partial(jnp.dot, preferred_element_type=F32)
    dn = functools.partial(lax.dot_general, dimension_numbers=NT_DIMS, preferred_element_type=F32)
    o_parts, lse_parts = [], []
    for g, s_ref in enumerate((s0_ref, s1_ref, s2_ref)):
        dil = ATTN_DILATIONS[g]
        nk = ATTN_WINDOWS[g] // dil
        lb = s_ref.shape[1]
        q = q_ref[:, g * gw:(g + 1) * gw]
        k_new = kvn[:, g * gw:(g + 1) * gw]
        v_new = kvn[:, attn_w + g * gw:attn_w + (g + 1) * gw]
        qblk = jnp.where(diag, jnp.broadcast_to(q, (rows, gw)), 0.0)
        q1, q2 = _split2(qblk)
        k1, k2 = _split2(s_ref[:gw, :])
        s = (d(q1, k1) + d(q2, k1) + d(q1, k2)) * scale
        back = lb - lax.broadcasted_iota(jnp.int32, (rows, lb), 1)
        valid = ((back & (dil - 1)) == 0) & (back <= dil * nk)
        s = jnp.where(valid, s, -jnp.inf)
        s_new = jnp.sum(qblk * k_new, axis=1, keepdims=True) * scale
        m = jnp.maximum(jnp.max(s, axis=1, keepdims=True), s_new)
        p = jnp.exp(s - m)
        p_new = jnp.exp(s_new - m)
        den = jnp.sum(p, axis=1, keepdims=True) + p_new
        lse = m + jnp.log(den)
        p1, p2 = _split2(p)
        v1, v2 = _split2(s_ref[gw:, :])
        o_full = (dn(p1, v1) + dn(p2, v1) + dn(p1, v2) + p_new * v_new) / den
        o_parts.append(jnp.sum(jnp.where(diag, o_full, 0.0), axis=0, keepdims=True))
        lse_parts.append(jnp.sum(jnp.where(diag, jnp.broadcast_to(lse, (rows, gw)), 0.0), axis=0, keepdims=True))
    o_ref[...] = jnp.concatenate(o_parts, axis=1)
    lse_ref[...] = jnp.concatenate(lse_parts, axis=1)


def _attn_sample(qkvp_s, kvf_s, states, *, attn_w, gw):
    bs = qkvp_s.shape[0]
    views, specs = [], []
    for g, st in enumerate(states):
        dil, win = ATTN_DILATIONS[g], ATTN_WINDOWS[g]
        lb = st.shape[1]
        assert lb == win and (dil & (dil - 1)) == 0 and st.shape[0] == bs
        views.append(jnp.transpose(st, (0, 2, 3, 4, 1)).reshape(bs, 2 * gw, lb))
        specs.append(pl.BlockSpec((None, 2 * gw, lb), lambda b: (b, 0, 0)))
    row3 = lambda w: pl.BlockSpec((None, 1, w), lambda b: (b, 0, 0))
    o, lse = pl.pallas_call(
        functools.partial(_attn_sample_kernel, attn_w=attn_w, gw=gw),
        grid=(bs,),
        in_specs=[row3(qkvp_s.shape[1]), row3(2 * attn_w), *specs],
        out_specs=[row3(attn_w), row3(attn_w)],
        out_shape=[jax.ShapeDtypeStruct((bs, 1, attn_w), F32),
                   jax.ShapeDtypeStruct((bs, 1, attn_w), F32)],
        compiler_params=_cparams(("arbitrary",)),
        name="attn_sample",
    )(qkvp_s.reshape(bs, 1, -1), kvf_s.reshape(bs, 1, 2 * attn_w), *views)
    return o.reshape(bs, attn_w), lse.reshape(bs, attn_w)


def _sgu_kernel(u_ref, va_ref, w_ref, bias_ref, g_ref, o_ref, *, cps):
    ng = w_ref.shape[0]
    ti = lax.broadcasted_iota(jnp.int32, (CHUNK, CHUNK), 0)
    si = lax.broadcasted_iota(jnp.int32, (CHUNK, CHUNK), 1)
    wt = [jnp.where(ti >= si, w_ref[g], 0.0).astype(BF16) for g in range(ng)]
    cw = va_ref.shape[1] // ng
    for c in range(cps):
        rows = slice(c * CHUNK, (c + 1) * CHUNK)
        va = va_ref[rows, :].astype(BF16)
        f = jnp.concatenate(
            [jnp.dot(wt[g], va[:, g * cw:(g + 1) * cw], preferred_element_type=F32) for g in range(ng)],
            axis=1) + bias_ref[...]
        s = u_ref[rows, :].astype(F32) * f
        ms = jnp.mean(s * s, axis=-1, keepdims=True)
        o_ref[rows, :] = (s * lax.rsqrt(ms + EPS) * g_ref[...]).astype(BF16)


def _sgu_prompt(u, va, w_sp, bias_tc, g_a, *, cps):
    n, w = u.shape
    tm = cps * CHUNK
    assert n % tm == 0
    row = lambda i: (i, 0)
    return pl.pallas_call(
        functools.partial(_sgu_kernel, cps=cps),
        grid=(n // tm,),
        in_specs=[pl.BlockSpec((tm, w), row), pl.BlockSpec((tm, w), row),
                  pl.BlockSpec(w_sp.shape, lambda i: (0, 0, 0)),
                  pl.BlockSpec((CHUNK, w), lambda i: (0, 0)),
                  pl.BlockSpec((1, w), lambda i: (0, 0))],
        out_specs=pl.BlockSpec((tm, w), row),
        out_shape=jax.ShapeDtypeStruct((n, w), BF16),
        compiler_params=_cparams(("arbitrary",)),
        name="sgu_prompt",
    )(u, va, w_sp, bias_tc, g_a)


def _sgu_sample_kernel(u_ref, va_ref, w0_ref, b0_ref, g_ref, o_ref):
    s = u_ref[...].astype(F32) * (va_ref[...] * w0_ref[...] + b0_ref[...])
    ms = jnp.mean(s * s, axis=-1, keepdims=True)
    o_ref[...] = (s * lax.rsqrt(ms + EPS) * g_ref[...]).astype(BF16)


def _sgu_sample(u, va, w0, b0, g_a):
    n, w = u.shape
    full = pl.BlockSpec((n, w), lambda i: (0, 0))
    vec = pl.BlockSpec((1, w), lambda i: (0, 0))
    return pl.pallas_call(
        _sgu_sample_kernel, grid=(1,),
        in_specs=[full, full, vec, vec, vec], out_specs=full,
        out_shape=jax.ShapeDtypeStruct((n, w), BF16),
        name="sgu_sample",
    )(u, va, w0, b0, g_a)


def _out_proj_kernel(*refs, aliased):
    ng = N_GROUPS
    o_refs, lse_refs = refs[:ng], refs[ng:2 * ng]
    rest = refs[2 * ng:]
    if aliased:
        rest = rest[:7] + rest[10:]
    sgu_ref, x_ref, gb_ref, wout_ref, gffn_ref, wr_ref, br_ref, x2_ref, xnp_ref, lg_ref = rest
    d = functools.partial(jnp.dot, preferred_element_type=F32)
    gw = o_refs[0].shape[0] * LANES
    ls = [r[...] for r in lse_refs]
    m = functools.reduce(jnp.maximum, ls)
    es = [jnp.exp(l - m) for l in ls]
    inv = 1.0 / functools.reduce(jnp.add, es)
    hi = lax.broadcasted_iota(jnp.int32, (LANES, gw), 0)
    li = lax.broadcasted_iota(jnp.int32, (LANES, gw), 1) // HEAD_DIM
    spread = jnp.where(hi == li, 1.0, 0.0).astype(BF16)
    attn = []
    for g in range(ng):
        a1, a2 = _split2(es[g] * inv)
        attn.append(_load_slabs(o_refs[g]) * (d(a1, spread) + d(a2, spread)))
    ms = functools.reduce(jnp.add, [jnp.sum(a * a, axis=-1, keepdims=True) for a in attn]) / (ng * gw)
    rstd = lax.rsqrt(ms + EPS)
    y = jnp.dot(sgu_ref[...], wout_ref[ng * gw:, :], preferred_element_type=F32)
    for g in range(ng):
        a_n = (attn[g] * rstd * gb_ref[:, g * gw:(g + 1) * gw]).astype(BF16)
        y = y + jnp.dot(a_n, wout_ref[g * gw:(g + 1) * gw, :], preferred_element_type=F32)
    x2 = x_ref[...] + y
    x2_ref[...] = x2
    ms2 = jnp.mean(x2 * x2, axis=-1, keepdims=True)
    xn = x2 * lax.rsqrt(ms2 + EPS) * gffn_ref[...]
    x1, xr = _split2(xn)
    lg_ref[...] = d(x1, wr_ref[0]) + d(x1, wr_ref[1]) + d(xr, wr_ref[0]) + br_ref[...]
    xnp_ref[...] = xn


def _out_proj(o_l, lse_l, sgu_n, x, g_b, wout_bf16, g_ffn, wr_hl, b_r, *, tm, n_total, row0, prev=None):
    n, d = x.shape
    gw = o_l[0].shape[0] * LANES
    assert n % tm == 0 and row0 % tm == 0 and len(o_l) == N_GROUPS
    off = row0 // tm
    row = lambda i: (i, 0)
    orow = lambda i: (i + off, 0)
    const2 = lambda i: (0, 0)
    in_specs = [pl.BlockSpec((gw // LANES, tm, LANES), lambda i: (0, i, 0))] * N_GROUPS + [
        pl.BlockSpec((tm, LANES), row)] * N_GROUPS + [
        pl.BlockSpec((tm, sgu_n.shape[1]), row),
        pl.BlockSpec((tm, d), row), pl.BlockSpec((1, N_GROUPS * gw), const2),
        pl.BlockSpec((d, d), const2), pl.BlockSpec((1, d), const2),
        pl.BlockSpec((2, d, LANES), lambda i: (0, 0, 0)), pl.BlockSpec((1, LANES), const2),
    ]
    args = [*o_l, *lse_l, sgu_n, x, g_b, wout_bf16, g_ffn, wr_hl, b_r]
    aliases = {}
    if prev is not None:
        in_specs += [pl.BlockSpec(memory_space=pl.ANY)] * 3
        aliases = {len(args) + k: k for k in range(3)}
        args += list(prev)
    return pl.pallas_call(
        functools.partial(_out_proj_kernel, aliased=prev is not None),
        grid=(n // tm,),
        in_specs=in_specs,
        out_specs=[pl.BlockSpec((tm, d), orow), pl.BlockSpec((tm, d), orow),
                   pl.BlockSpec((tm, LANES), orow)],
        out_shape=[jax.ShapeDtypeStruct((n_total, d), F32),
                   jax.ShapeDtypeStruct((n_total, d), F32),
                   jax.ShapeDtypeStruct((n_total, LANES), F32)],
        input_output_aliases=aliases,
        compiler_params=_cparams(("arbitrary",)),
        name="out_proj",
    )(*args)


def _moe_kernel(be_ref, ns_ref, sb_ref, pro_ref, pref_ref, dst_ref, xn_hbm,
                wg_ref, wl_ref, bg_ref, bl_ref, wd_ref, bd_ref, out_hbm,
                xb_sc, acc_sc, wg_sc, wl_sc, wd_sc, sem_in, sem_out, *, nc):
    b = pl.program_id(0)
    c = pl.program_id(1)
    nb = pl.num_programs(0)
    ns = ns_ref[b]
    s0 = sb_ref[b]
    del be_ref
    per_it = xb_sc.shape[1]
    dm = xb_sc.shape[2]
    d = functools.partial(jnp.dot, preferred_element_type=F32)

    def row_in(tok, slot, part, k):
        return pltpu.make_async_copy(xn_hbm.at[pl.ds(tok, 1), :], xb_sc.at[slot * nc + part, pl.ds(k, 1), :],
                                     sem_in.at[slot])

    def sub_in_wait(slot):
        for _ in range(nc):
            pltpu.make_async_copy(xn_hbm.at[pl.ds(0, per_it), :], xb_sc.at[0], sem_in.at[slot]).wait()

    def sub_out_wait():
        pltpu.make_async_copy(acc_sc.at[0], out_hbm.at[pl.ds(0, MOE_SUB), :], sem_out).wait()

    def compute(s):
        it = c * ns + s
        slot = (s0 + MOE_NSUB + it // nc) % MOE_RING
        part = it % nc
        for k in range(per_it):
            row_in(pref_ref[0, it * per_it + k], slot, part, k).start()
        x0 = ((s0 + s) % MOE_RING) * nc
        x = xb_sc[pl.ds(x0, nc)].reshape(MOE_SUB, dm).astype(BF16)
        hg = d(x, wg_sc[...]) + bg_ref[...]
        hl = d(x, wl_sc[...]) + bl_ref[...]
        glu = jnp.minimum(hg, SWIGLU_LIMIT)
        lin = jnp.clip(hl, -SWIGLU_LIMIT, SWIGLU_LIMIT)
        a = glu * jax.nn.sigmoid(SWIGLU_ALPHA * glu) * (lin + 1.0)
        return d(a.astype(BF16), wd_sc[...])

    def scatter(s):
        a0 = s * MOE_SUB
        for k in range(MOE_SUB):
            pltpu.make_async_copy(acc_sc.at[s, pl.ds(k, 1), :],
                                  out_hbm.at[pl.ds(dst_ref[0, a0 + k], 1), :], sem_out).start()

    def out_wait(n_sub):
        def wait(s, carry):
            sub_out_wait()
            return carry
        lax.fori_loop(0, n_sub, wait, 0)

    @pl.when((b == 0) & (c == 0))
    def _():
        def start(i, carry):
            part = i // per_it
            pltpu.make_async_copy(xn_hbm.at[pl.ds(pro_ref[0, i], 1), :], xb_sc.at[part, pl.ds(i % per_it, 1), :],
                                  sem_in.at[part // nc]).start()
            return carry
        lax.fori_loop(0, MOE_TM, start, 0)

    @pl.when((b > 0) & (c == 0))
    def _():
        out_wait(ns_ref[b - 1])

    @pl.when(ns > 0)
    def _():
        @pl.when(c == 0)
        def _():
            for s in range(MOE_NSUB):
                @pl.when(s < ns)
                def _():
                    sub_in_wait((s0 + s) % MOE_RING)

        wg_sc[...] = wg_ref[...].astype(BF16)
        wl_sc[...] = wl_ref[...].astype(BF16)
        wd_sc[...] = wd_ref[...].astype(BF16)

        @pl.when(c == 0)
        def _():
            def body(s, carry):
                acc_sc[s] = compute(s) + bd_ref[...]
                return carry
            lax.fori_loop(0, ns, body, 0)

        @pl.when((c > 0) & (c < nc - 1))
        def _():
            def body(s, carry):
                acc_sc[s] += compute(s)
                return carry
            lax.fori_loop(0, ns, body, 0)

        @pl.when(c == nc - 1)
        def _():
            def finish(s):
                acc_sc[s] += compute(s)

            finish(0)

            def body(s, carry):
                finish(s)
                scatter(s - 1)
                return carry
            lax.fori_loop(1, ns, body, 0)
            scatter(ns - 1)

            @pl.when(b == nb - 1)
            def _():
                out_wait(ns)

    @pl.when((b == nb - 1) & (c == nc - 1))
    def _():
        total = sb_ref[nb - 1] + ns_ref[nb - 1]
        for k in range(MOE_NSUB):
            sub_in_wait((total + k) % MOE_RING)


def _moe(xnp, sched, w_up, b_up, w_down, b_down, *, n_rows_out):
    be, nsub, sub0, pro_tbl, pref_tbl, dst_tbl = sched
    n_tok, d = xnp.shape
    n_exp, _, ff2 = w_up.shape
    ff = ff2 // 2
    assert ff % MOE_TF == 0 and MOE_TF % LANES == 0
    nc = ff // MOE_TF
    assert nc > 2 and MOE_SUB % (8 * nc) == 0
    nbmax = be.shape[0]

    def cidx(b, c, ns):
        return jnp.where(ns[b] > 0, c, nc - 1)

    smem_tbl = lambda: pl.BlockSpec((None, 1, MOE_TM), lambda b, c, be, ns, sb: (b, 0, 0), memory_space=pltpu.SMEM)
    grid_spec = pltpu.PrefetchScalarGridSpec(
        num_scalar_prefetch=3,
        grid=(nbmax, nc),
        in_specs=[
            pl.BlockSpec((1, MOE_TM), lambda b, c, be, ns, sb: (0, 0), memory_space=pltpu.SMEM),
            smem_tbl(),
            smem_tbl(),
            pl.BlockSpec(memory_space=pl.ANY),
            pl.BlockSpec((None, d, MOE_TF), lambda b, c, be, ns, sb: (be[b], 0, cidx(b, c, ns))),
            pl.BlockSpec((None, d, MOE_TF), lambda b, c, be, ns, sb: (be[b], 0, nc + cidx(b, c, ns))),
            pl.BlockSpec((None, 1, MOE_TF), lambda b, c, be, ns, sb: (be[b], 0, cidx(b, c, ns))),
            pl.BlockSpec((None, 1, MOE_TF), lambda b, c, be, ns, sb: (be[b], 0, nc + cidx(b, c, ns))),
            pl.BlockSpec((None, MOE_TF, d), lambda b, c, be, ns, sb: (be[b], cidx(b, c, ns), 0)),
            pl.BlockSpec((None, 1, d), lambda b, c, be, ns, sb: (be[b], 0, 0)),
        ],
        out_specs=pl.BlockSpec(memory_space=pl.ANY),
        scratch_shapes=[
            pltpu.VMEM((MOE_RING * nc, MOE_SUB // nc, d), F32),
            pltpu.VMEM((MOE_NSUB, MOE_SUB, d), F32),
            pltpu.VMEM((d, MOE_TF), BF16),
            pltpu.VMEM((d, MOE_TF), BF16),
            pltpu.VMEM((MOE_TF, d), BF16),
            pltpu.SemaphoreType.DMA((MOE_RING,)),
            pltpu.SemaphoreType.DMA,
        ],
    )
    return pl.pallas_call(
        functools.partial(_moe_kernel, nc=nc),
        grid_spec=grid_spec,
        out_shape=jax.ShapeDtypeStruct((n_rows_out, d), F32),
        compiler_params=_cparams(("arbitrary", "arbitrary")),
        name="moe",
    )(be, nsub, sub0, pro_tbl, pref_tbl, dst_tbl, xnp, w_up, w_up,
      b_up.reshape(n_exp, 1, ff2), b_up.reshape(n_exp, 1, ff2), w_down, b_down.reshape(n_exp, 1, d))


def _moe_schedule(logits, n_exp, n_pad):
    n = logits.shape[0]
    i32 = jnp.int32
    top_v, top_i = lax.top_k(logits, TOP_K)
    gates = jax.nn.softmax(top_v, axis=-1)
    na = n * TOP_K
    flat_e = top_i.reshape(-1).astype(i32)
    order = jnp.sort(flat_e * na + jnp.arange(na, dtype=i32)) % na
    counts = jnp.bincount(flat_e, length=n_exp).astype(i32)
    starts = jnp.cumsum(counts) - counts
    ns_e = (counts + MOE_SUB - 1) // MOE_SUB
    nblk_e = (ns_e + MOE_NSUB - 1) // MOE_NSUB
    spb_e = -(-ns_e // jnp.maximum(nblk_e, 1))
    sub_ends = jnp.cumsum(ns_e)
    sub_starts = sub_ends - ns_e
    total_subs = sub_ends[-1]
    bends = jnp.cumsum(nblk_e)
    total_blocks = bends[-1]

    nsub_max = -(-na // MOE_SUB) + n_exp
    nbmax = -(-nsub_max // MOE_NSUB) + n_exp
    bid = jnp.arange(nbmax, dtype=i32)
    used = bid < total_blocks
    owner = lambda ids, ends: jnp.minimum(jnp.sum(ids[:, None] >= ends[None, :], axis=1), n_exp - 1).astype(i32)
    e_b = owner(jnp.minimum(bid, total_blocks - 1), bends)
    jb = bid - (bends[e_b] - nblk_e[e_b])
    sub0 = jnp.where(used, sub_starts[e_b] + jb * spb_e[e_b], total_subs).astype(i32)
    nsub = jnp.where(used, jnp.clip(ns_e[e_b] - jb * spb_e[e_b], 0, spb_e[e_b]), 0).astype(i32)

    gs = nsub_max + 2 * MOE_NSUB
    gid = jnp.arange(gs, dtype=i32)
    e_g = owner(gid, sub_ends)
    lane = jnp.arange(MOE_SUB, dtype=i32)[None, :]
    within = (gid - sub_starts[e_g])[:, None] * MOE_SUB + lane
    valid = (gid < total_subs)[:, None] & (within < counts[e_g][:, None])
    a = order[jnp.clip(starts[e_g][:, None] + within, 0, na - 1)]
    tok_f = jnp.where(valid, a // TOP_K, 0).astype(i32).reshape(-1)
    dst_f = jnp.where(valid, (a % TOP_K) * n_pad + a // TOP_K, n + lane).astype(i32).reshape(-1)

    own = sub0[:, None] * MOE_SUB + jnp.arange(MOE_TM, dtype=i32)[None, :]
    dst_tbl = dst_f[own].reshape(nbmax, 1, MOE_TM)
    pref_tbl = tok_f[own + MOE_TM].reshape(nbmax, 1, MOE_TM)
    pro_tbl = tok_f[:MOE_TM].reshape(1, MOE_TM)
    return (e_b, nsub, sub0, pro_tbl, pref_tbl, dst_tbl), gates


def _final_kernel(x2_ref, y4_ref, gate_ref, g_ref, o_ref):
    gates = gate_ref[...]
    y = y4_ref[0] * gates[:, 0:1]
    for k in range(1, TOP_K):
        y = y + y4_ref[k] * gates[:, k:k + 1]
    x = x2_ref[...] + y
    ms = jnp.mean(x * x, axis=-1, keepdims=True)
    o_ref[...] = x * lax.rsqrt(ms + EPS) * g_ref[...]


def _final(x2, y4, gates, g, *, tm, row0, n):
    d = x2.shape[1]
    assert n % tm == 0 and row0 % tm == 0
    off = row0 // tm
    return pl.pallas_call(
        _final_kernel, grid=(n // tm,),
        in_specs=[pl.BlockSpec((tm, d), lambda i: (i + off, 0)),
                  pl.BlockSpec((TOP_K, tm, d), lambda i: (0, i + off, 0)),
                  pl.BlockSpec((tm, TOP_K), lambda i: (i + off, 0)),
                  pl.BlockSpec((1, d), lambda i: (0, 0))],
        out_specs=pl.BlockSpec((tm, d), lambda i: (i, 0)),
        out_shape=jax.ShapeDtypeStruct((n, d), F32),
        compiler_params=_cparams(("arbitrary",)),
        name="final",
    )(x2, y4, gates, g)


def kernel(x_prompt, x_sample, state_b0_kv, state_b1_kv, state_b2_kv, norm_mix_g, w_in, ln_va_g, ln_va_b,
           w_spatial, b_spatial, norm_out_b_g, norm_out_a_g, w_out, norm_ffn_g, w_router, b_router,
           w_up, b_up, w_down, b_down, norm_final_g):
    bp, sp, d = x_prompt.shape
    bs, ss, _ = x_sample.shape
    depth = w_in.shape[0]
    assert depth == 1 and ss == 1
    hpg = state_b0_kv.shape[4]
    gw = hpg * HEAD_DIM
    attn_w = N_GROUPS * gw
    sgu_w = ln_va_g.shape[1]
    n_exp = w_router.shape[2]
    npr, nsm = bp * sp, bs * ss
    n_total = npr + nsm
    states = (state_b0_kv[0], state_b1_kv[0], state_b2_kv[0])

    w_in_b = w_in[0].astype(BF16)
    w_out_b = w_out[0].astype(BF16)
    g_mix = norm_mix_g[0].reshape(1, d)
    ln_g, ln_b = ln_va_g[0].reshape(1, sgu_w), ln_va_b[0].reshape(1, sgu_w)
    g_b = norm_out_b_g[0].reshape(1, attn_w)
    g_a = norm_out_a_g[0].reshape(1, sgu_w)
    g_ffn = norm_ffn_g[0].reshape(1, d)
    ng = w_spatial.shape[1]
    cw = sgu_w // ng
    bias_tc = jnp.repeat(b_spatial[0].T, cw, axis=1)
    w00 = jnp.repeat(w_spatial[0][:, 0, 0], cw).reshape(1, sgu_w)
    b00 = jnp.repeat(b_spatial[0][:, 0], cw).reshape(1, sgu_w)
    wr = jnp.pad(w_router[0], ((0, 0), (0, LANES - n_exp)))
    wr_hl = jnp.stack(_split2(wr), 0)
    b_r = jnp.pad(b_router[0], (0, LANES - n_exp)).reshape(1, LANES)

    tabs_p = _rope_tables(jnp.tile(jnp.arange(sp), bp))
    tabs_s = _rope_tables(jnp.tile(PAST_LEN + jnp.arange(ss), bs))

    xp = x_prompt.reshape(npr, d)
    xs = x_sample.reshape(nsm, d)
    qkvp_p, u_p, va_p, *kvt_p = _in_proj(xp, g_mix, w_in_b, *tabs_p, ln_g, ln_b, tm=1024,
                                         attn_w=attn_w, sgu_w=sgu_w, seqs=(bp, sp))
    qkvp_s, u_s, va_s, kvf_s = _in_proj(xs, g_mix, w_in_b, *tabs_s, ln_g, ln_b, tm=nsm,
                                        attn_w=attn_w, sgu_w=sgu_w)

    o_p, lse_p = zip(*[_attn_prompt(qkvp_p, g, bp=bp, sp=sp, gw=gw) for g in range(N_GROUPS)])
    rows_s = jnp.transpose(qkvp_s, (1, 0, 2)).reshape(nsm, -1)
    o_s, lse_s = _attn_sample(rows_s, kvf_s, states, attn_w=attn_w, gw=gw)
    split = lambda t: [jnp.transpose(t[:, g * gw:(g + 1) * gw].reshape(nsm, gw // LANES, LANES), (1, 0, 2))
                       for g in range(N_GROUPS)]

    sgu_p = _sgu_prompt(u_p, va_p, w_spatial[0], bias_tc, g_a, cps=4)
    sgu_s = _sgu_sample(u_s, va_s, w00, b00, g_a)

    prev = _out_proj(o_p, lse_p, sgu_p, xp, g_b, w_out_b, g_ffn, wr_hl, b_r, tm=256, n_total=n_total, row0=0)
    lse_c = lse_s.reshape(nsm, N_GROUPS, hpg, HEAD_DIM)[..., 0]
    lse_sl = [jnp.pad(lse_c[:, g], ((0, 0), (0, LANES - hpg))) for g in range(N_GROUPS)]
    x2, xnp, logits = _out_proj(split(o_s), lse_sl, sgu_s, xs, g_b, w_out_b, g_ffn, wr_hl, b_r,
                                tm=nsm, n_total=n_total, row0=npr, prev=prev)

    n_pad = n_total + MOE_SUB
    assert n_pad % 8 == 0
    sched, gates = _moe_schedule(logits[:, :n_exp], n_exp, n_pad)
    y4 = _moe(xnp, sched, w_up[0], b_up[0], w_down[0], b_down[0], n_rows_out=TOP_K * n_pad)
    y4 = y4.reshape(TOP_K, n_pad, d)
    g_fin = norm_final_g.reshape(1, d)
    y_p = _final(x2, y4, gates, g_fin, tm=256, row0=0, n=npr).reshape(bp, sp, d)
    y_s = _final(x2, y4, gates, g_fin, tm=nsm, row0=npr, n=nsm).reshape(bs, ss, d)

    kv_prompt = [jnp.transpose(t[:bp].reshape(bp, 2, hpg, HEAD_DIM, t.shape[2]), (0, 4, 1, 2, 3))[None]
                 for t in kvt_p]
    ksm = kvf_s.reshape(bs, ss, 2, N_GROUPS, hpg, HEAD_DIM)
    kv_sample = [ksm[:, :, :, g][None] for g in range(N_GROUPS)]
    v_sgu = va_s.reshape(1, bs, ss, sgu_w)
    return (y_p, y_s, *kv_prompt, *kv_sample, v_sgu)
```
